```python
import jax
import jax.numpy as jnp
from jax import lax
import numpy as np

D_MODEL = 2048
BATCH = 4
SEQ = 4096
DEPTH = 2

GRID_W = 64
CTX_LEN = 256
Q_BLOCK = 128
ROPE_THETA = 10000.0
NORM_EPS = 1e-6
MLA_HEADS = 8
MLA_Q_RANK = 512
MLA_KV_RANK = 256
MLA_NOPE_DIM = 128
MLA_ROPE_DIM = 64
MLA_V_DIM = 128
MLA_QK_DIM = MLA_NOPE_DIM + MLA_ROPE_DIM
GQA_HEADS = 8
GQA_KV_HEADS = 2
GQA_HEAD_DIM = 128
ATTN_SPLITS = (
    MLA_Q_RANK,
    MLA_Q_RANK + MLA_KV_RANK,
    MLA_Q_RANK + MLA_KV_RANK + MLA_ROPE_DIM,
    MLA_Q_RANK + MLA_KV_RANK + MLA_ROPE_DIM + GQA_HEADS * GQA_HEAD_DIM,
    MLA_Q_RANK + MLA_KV_RANK + MLA_ROPE_DIM + (GQA_HEADS + GQA_KV_HEADS) * GQA_HEAD_DIM,
)
ATTN_IN_WIDTH = MLA_Q_RANK + MLA_KV_RANK + MLA_ROPE_DIM + (GQA_HEADS + 2 * GQA_KV_HEADS) * GQA_HEAD_DIM
ATTN_OUT_WIDTH = MLA_HEADS * MLA_V_DIM + GQA_HEADS * GQA_HEAD_DIM
CFM_WIDTH = 1024
CFM_KERNEL = 31
SC_WIDTH = 1024
SC_KERNEL = 3
CONV_SPLITS = (CFM_WIDTH, 2 * CFM_WIDTH, 2 * CFM_WIDTH + SC_WIDTH, 2 * CFM_WIDTH + 2 * SC_WIDTH)
CONV_IN_WIDTH = 2 * CFM_WIDTH + 3 * SC_WIDTH
CONV_OUT_WIDTH = CFM_WIDTH + SC_WIDTH
PEER_HEADS = 8
PEER_KEY_DIM = 256
PEER_N_KEYS = 128
PEER_N_EXPERTS = PEER_N_KEYS * PEER_N_KEYS
PEER_TOPK = 16
PEER_CHUNK = 128

kernel_name = 'hybrid_mla_gqa_conformer_shortconv_peer_dit'


def rms_norm(x, g):
    xf = x.astype(jnp.float32)
    y = xf * lax.rsqrt(jnp.mean(xf * xf, axis=-1, keepdims=True) + NORM_EPS)
    return (y * g.astype(jnp.float32)).astype(x.dtype)


def layer_norm(x, g, b):
    xf = x.astype(jnp.float32)
    mu = jnp.mean(xf, axis=-1, keepdims=True)
    xc = xf - mu
    y = xc * lax.rsqrt(jnp.mean(xc * xc, axis=-1, keepdims=True) + NORM_EPS)
    return (y * g.astype(jnp.float32) + b.astype(jnp.float32)).astype(x.dtype)


def modulate(h, shift, scale):
    return h * (1 + scale) + shift


def axial_rope_tables(n_rows, d_rot):
    row = jnp.broadcast_to(jnp.arange(n_rows, dtype=jnp.float32)[:, None], (n_rows, GRID_W)).reshape(-1)
    col = jnp.broadcast_to(jnp.arange(GRID_W, dtype=jnp.float32)[None, :], (n_rows, GRID_W)).reshape(-1)
    quarter = d_rot // 4
    inv_freq = ROPE_THETA ** (-jnp.arange(quarter, dtype=jnp.float32) / quarter)
    ar = row[:, None] * inv_freq
    ac = col[:, None] * inv_freq
    ang = jnp.concatenate([ar, ar, ac, ac], axis=-1)[:, None, :]
    return jnp.cos(ang), jnp.sin(ang)


def apply_rope(x, cos, sin):
    xf = x.astype(jnp.float32)
    r1, r2, c1, c2 = jnp.split(xf, 4, axis=-1)
    rot = jnp.concatenate([-r2, r1, -c2, c1], axis=-1)
    return (xf * cos + rot * sin).astype(x.dtype)


def block_attention(q, k, v, scale):
    b, sq, hk, g, dk = q.shape
    nblk = sq // Q_BLOCK
    qb = jnp.moveaxis(q.reshape(b, nblk, Q_BLOCK, hk, g, dk), 1, 0)

    def one_block(qi):
        s = jnp.einsum('bqhgd,bthd->bhgqt', qi, k).astype(jnp.float32) * scale
        p = jax.nn.softmax(s, axis=-1).astype(v.dtype)
        return jnp.einsum('bhgqt,bthe->bqhge', p, v)

    o = lax.map(one_block, qb)
    return jnp.moveaxis(o, 0, 1).reshape(b, sq, -1)


def attn_queries(parts, mla_q_norm_g, mla_w_uq, gqa_q_norm_g, rope):
    cq, _, _, gq, _, _ = parts
    b, s = cq.shape[:2]
    q = (rms_norm(cq, mla_q_norm_g) @ mla_w_uq).reshape(b, s, MLA_HEADS, MLA_QK_DIM)
    q_nope, q_rope = q[..., :MLA_NOPE_DIM], q[..., MLA_NOPE_DIM:]
    gq = rms_norm(gq.reshape(b, s, GQA_HEADS, GQA_HEAD_DIM), gqa_q_norm_g)
    if rope is not None:
        q_rope = apply_rope(q_rope, rope[0], rope[1])
        gq = apply_rope(gq, rope[2], rope[3])
    mla_q = jnp.concatenate([q_nope, q_rope], axis=-1)[:, :, :, None, :]
    gqa_q = gq.reshape(b, s, GQA_KV_HEADS, GQA_HEADS // GQA_KV_HEADS, GQA_HEAD_DIM)
    return mla_q, gqa_q


def attn_keys_values(parts, mla_kv_norm_g, mla_w_ukv, gqa_k_norm_g, rope):
    _, ckv, kr, _, gk, gv = parts
    b, s = ckv.shape[:2]
    kv = (rms_norm(ckv, mla_kv_norm_g) @ mla_w_ukv).reshape(b, s, MLA_HEADS, MLA_NOPE_DIM + MLA_V_DIM)
    k_nope, mla_v = kv[..., :MLA_NOPE_DIM], kv[..., MLA_NOPE_DIM:]
    kr = kr[:, :, None, :]
    gk = rms_norm(gk.reshape(b, s, GQA_KV_HEADS, GQA_HEAD_DIM), gqa_k_norm_g)
    gv = gv.reshape(b, s, GQA_KV_HEADS, GQA_HEAD_DIM)
    if rope is not None:
        kr = apply_rope(kr, rope[0], rope[1])
        gk = apply_rope(gk, rope[2], rope[3])
    mla_k = jnp.concatenate([k_nope, jnp.broadcast_to(kr, (b, s, MLA_HEADS, MLA_ROPE_DIM))], axis=-1)
    return mla_k, mla_v, gk, gv


def depthwise_conv(x, w):
    pad = (w.shape[0] - 1) // 2
    return lax.conv_general_dilated(
        x, w[:, None, :].astype(x.dtype), window_strides=(1,), padding=[(pad, pad)],
        dimension_numbers=('NWC', 'WIO', 'NWC'), feature_group_count=x.shape[-1])


def conv_mix(h, w_in, cfm_w, cfm_b, ln_g, ln_b, sc_w, w_out):
    a, a_gate, g_b, g_c, u = jnp.split(h @ w_in, CONV_SPLITS, axis=-1)
    yc = a * jax.nn.sigmoid(a_gate)
    yc = jax.nn.silu(layer_norm(depthwise_conv(yc, cfm_w) + cfm_b, ln_g, ln_b))
    yd = g_b * depthwise_conv(g_c * u, sc_w)
    return jnp.concatenate([yc, yd], axis=-1) @ w_out


def peer_ffn(h, w_q, key1, key2, u, v):
    shape = h.shape
    tokens = h.reshape(-1, PEER_CHUNK, shape[-1])
    half = PEER_KEY_DIM // 2

    def chunk(hb):
        t = hb.shape[0]
        q = (hb @ w_q).reshape(t, PEER_HEADS, PEER_KEY_DIM)
        s1 = jnp.einsum('thd,kd->thk', q[..., :half], key1).astype(jnp.float32)
        s2 = jnp.einsum('thd,kd->thk', q[..., half:], key2).astype(jnp.float32)
        v1, i1 = lax.top_k(s1, PEER_TOPK)
        v2, i2 = lax.top_k(s2, PEER_TOPK)
        cand = (v1[..., :, None] + v2[..., None, :]).reshape(t, PEER_HEADS, PEER_TOPK * PEER_TOPK)
        cand_id = (i1[..., :, None] * PEER_N_KEYS + i2[..., None, :]).reshape(t, PEER_HEADS, PEER_TOPK * PEER_TOPK)
        best, pos = lax.top_k(cand, PEER_TOPK)
        ids = jnp.take_along_axis(cand_id, pos, axis=-1)
        gate = jax.nn.softmax(best, axis=-1)
        u_sel = jnp.take(u, ids, axis=0)
        act = jax.nn.gelu(jnp.einsum('thkd,td->thk', u_sel, hb).astype(jnp.float32)) * gate
        return jnp.einsum('thk,thkd->td', act.astype(hb.dtype), jnp.take(v, ids, axis=0))

    return lax.map(chunk, tokens).reshape(shape)


def setup_inputs(seed: int = 0) -> dict:
    key = jax.random.key(seed)
    ks = jax.random.split(key, 29)
    n_attn = (DEPTH + 1) // 2
    n_conv = DEPTH // 2
    D = D_MODEL

    def nrm(k, shape, scale):
        return jax.random.normal(k, shape, jnp.float32) * scale

    def gain(k, shape):
        return 1.0 + 0.02 * jax.random.normal(k, shape, jnp.float32)

    return {
        'x': nrm(ks[0], (BATCH, SEQ, D), 1.0),
        'c': nrm(ks[1], (BATCH, D), 1.0),
        'ctx': nrm(ks[2], (BATCH, CTX_LEN, D), 1.0),
        'c_ctx': nrm(ks[3], (D,), 1.0),
        'ada_w': nrm(ks[4], (DEPTH, D, 6 * D), 0.5 * D ** -0.5),
        'ada_b': nrm(ks[5], (DEPTH, 6 * D), 0.01),
        'norm1_g': gain(ks[6], (DEPTH, D)),
        'norm2_g': gain(ks[7], (DEPTH, D)),
        'attn_w_in': nrm(ks[8], (n_attn, D, ATTN_IN_WIDTH), D ** -0.5),
        'mla_q_norm_g': gain(ks[9], (n_attn, MLA_Q_RANK)),
        'mla_w_uq': nrm(ks[10], (n_attn, MLA_Q_RANK, MLA_HEADS * MLA_QK_DIM), MLA_Q_RANK ** -0.5),
        'mla_kv_norm_g': gain(ks[11], (n_attn, MLA_KV_RANK)),
        'mla_w_ukv': nrm(ks[12], (n_attn, MLA_KV_RANK, MLA_HEADS * (MLA_NOPE_DIM + MLA_V_DIM)), MLA_KV_RANK ** -0.5),
        'gqa_q_norm_g': gain(ks[13], (n_attn, GQA_HEAD_DIM)),
        'gqa_k_norm_g': gain(ks[14], (n_attn, GQA_HEAD_DIM)),
        'attn_w_out': nrm(ks[15], (n_attn, ATTN_OUT_WIDTH, D), ATTN_OUT_WIDTH ** -0.5),
        'conv_w_in': nrm(ks[16], (n_conv, D, CONV_IN_WIDTH), D ** -0.5),
        'cfm_dw_w': nrm(ks[17], (n_conv, CFM_KERNEL, CFM_WIDTH), CFM_KERNEL ** -0.5),
        'cfm_dw_b': nrm(ks[18], (n_conv, CFM_WIDTH), 0.01),
        'cfm_ln_g': gain(ks[19], (n_conv, CFM_WIDTH)),
        'cfm_ln_b': nrm(ks[20], (n_conv, CFM_WIDTH), 0.01),
        'sc_dw_w': nrm(ks[21], (n_conv, SC_KERNEL, SC_WIDTH), SC_KERNEL ** -0.5),
        'conv_w_out': nrm(ks[22], (n_conv, CONV_OUT_WIDTH, D), CONV_OUT_WIDTH ** -0.5),
        'peer_w_q': nrm(ks[23], (DEPTH, D, PEER_HEADS * PEER_KEY_DIM), D ** -0.5),
        'peer_key1': nrm(ks[24], (DEPTH, PEER_N_KEYS, PEER_KEY_DIM // 2), (PEER_KEY_DIM // 2) ** -0.5),
        'peer_key2': nrm(ks[25], (DEPTH, PEER_N_KEYS, PEER_KEY_DIM // 2), (PEER_KEY_DIM // 2) ** -0.5),
        'peer_u': nrm(ks[26], (DEPTH, PEER_N_EXPERTS, D), D ** -0.5),
        'peer_v': nrm(ks[27], (DEPTH, PEER_N_EXPERTS, D), 1.0),
        'final_norm_g': gain(ks[28], (D,)),
    }


def reference(x, c, ctx, c_ctx, ada_w, ada_b, norm1_g, norm2_g, attn_w_in, mla_q_norm_g, mla_w_uq,
              mla_kv_norm_g, mla_w_ukv, gqa_q_norm_g, gqa_k_norm_g, attn_w_out, conv_w_in, cfm_dw_w,
              cfm_dw_b, cfm_ln_g, cfm_ln_b, sc_dw_w, conv_w_out, peer_w_q, peer_key1, peer_key2,
              peer_u, peer_v, final_norm_g):
    n_rows = x.shape[1] // GRID_W
    rope = axial_rope_tables(n_rows, MLA_ROPE_DIM) + axial_rope_tables(n_rows, GQA_HEAD_DIM)
    mla_scale = MLA_QK_DIM ** -0.5
    gqa_scale = GQA_HEAD_DIM ** -0.5
    for i in range(DEPTH):
        mod_l = jnp.split((jax.nn.silu(c) @ ada_w[i] + ada_b[i])[:, None, :], 6, axis=-1)
        mod_c = jnp.split((jax.nn.silu(c_ctx) @ ada_w[i] + ada_b[i])[None, None, :], 6, axis=-1)
        ctx_live = any(j % 2 == 0 for j in range(i + 1, DEPTH))
        h = modulate(rms_norm(x, norm1_g[i]), mod_l[0], mod_l[1])
        if i % 2 == 0:
            a = i // 2
            hc = modulate(rms_norm(ctx, norm1_g[i]), mod_c[0], mod_c[1])
            parts = jnp.split(h @ attn_w_in[a], ATTN_SPLITS, axis=-1)
            parts_c = jnp.split(hc @ attn_w_in[a], ATTN_SPLITS, axis=-1)
            mq, gq = attn_queries(parts, mla_q_norm_g[a], mla_w_uq[a], gqa_q_norm_g[a], rope)
            mk, mv, gk, gv = attn_keys_values(parts, mla_kv_norm_g[a], mla_w_ukv[a], gqa_k_norm_g[a], rope)
            mkc, mvc, gkc, gvc = attn_keys_values(parts_c, mla_kv_norm_g[a], mla_w_ukv[a], gqa_k_norm_g[a], None)
            o = jnp.concatenate([
                block_attention(mq, jnp.concatenate([mkc, mk], axis=1), jnp.concatenate([mvc, mv], axis=1), mla_scale),
                block_attention(gq, jnp.concatenate([gkc, gk], axis=1), jnp.concatenate([gvc, gv], axis=1), gqa_scale),
            ], axis=-1)
            x = x + mod_l[2] * (o @ attn_w_out[a])
            if ctx_live:
                mqc, gqc = attn_queries(parts_c, mla_q_norm_g[a], mla_w_uq[a], gqa_q_norm_g[a], None)
                oc = jnp.concatenate([
                    block_attention(mqc, mkc, mvc, mla_scale),
                    block_attention(gqc, gkc, gvc, gqa_scale),
                ], axis=-1)
                ctx = ctx + mod_c[2] * (oc @ attn_w_out[a])
        else:
            m = i // 2
            x = x + mod_l[2] * conv_mix(h, conv_w_in[m], cfm_dw_w[m], cfm_dw_b[m], cfm_ln_g[m], cfm_ln_b[m],
                                        sc_dw_w[m], conv_w_out[m])
            if ctx_live:
                hc = modulate(rms_norm(ctx, norm1_g[i]), mod_c[0], mod_c[1])
                ctx = ctx + mod_c[2] * conv_mix(hc, conv_w_in[m], cfm_dw_w[m], cfm_dw_b[m], cfm_ln_g[m],
                                                cfm_ln_b[m], sc_dw_w[m], conv_w_out[m])
        h2 = modulate(rms_norm(x, norm2_g[i]), mod_l[3], mod_l[4])
        x = x + mod_l[5] * peer_ffn(h2, peer_w_q[i], peer_key1[i], peer_key2[i], peer_u[i], peer_v[i])
        if ctx_live:
            h2c = modulate(rms_norm(ctx, norm2_g[i]), mod_c[3], mod_c[4])
            ctx = ctx + mod_c[5] * peer_ffn(h2c, peer_w_q[i], peer_key1[i], peer_key2[i], peer_u[i], peer_v[i])
    return rms_norm(x, final_norm_g)
```

```python
import functools

import jax
import jax.numpy as jnp
from jax import lax
from jax.experimental import pallas as pl
from jax.experimental.pallas import tpu as pltpu

F32 = jnp.float32
BF16 = jnp.bfloat16

GRID_W = 64
ROPE_THETA = 10000.0
NORM_EPS = 1e-6
MLA_HEADS = 8
MLA_Q_RANK = 512
MLA_KV_RANK = 256
MLA_NOPE_DIM = 128
MLA_ROPE_DIM = 64
MLA_V_DIM = 128
MLA_QK_DIM = MLA_NOPE_DIM + MLA_ROPE_DIM
GQA_HEADS = 8
GQA_KV_HEADS = 2
GQA_HEAD_DIM = 128
CFM_WIDTH = 1024
CFM_KERNEL = 31
SC_WIDTH = 1024
SC_KERNEL = 3
PEER_HEADS = 8
PEER_KEY_DIM = 256
PEER_N_KEYS = 128
PEER_TOPK = 16
PEER_PICKS = PEER_HEADS * PEER_TOPK

LANES = 128
SUBLANES = 8
VMEM_LIMIT = 48 * 1024 * 1024

ROW_TILE = 256
ATTN_Q_TILE = 256
CONV_TILE = 256
CONV_HALO = 16
GATHER_TOKENS = 8
GATHER_UNROLL = 16


def _cparams(*sem):
    return pltpu.CompilerParams(dimension_semantics=sem, vmem_limit_bytes=VMEM_LIMIT)


def _mod_kernel(c_ref, w_ref, b_ref, o_ref):
    c = c_ref[...]
    a = (c * jax.nn.sigmoid(c)).astype(BF16)
    o_ref[...] = jnp.dot(a, w_ref[...].astype(BF16), preferred_element_type=F32) + b_ref[...]


def _modulation(cc, w, b):
    r, d = cc.shape
    n = w.shape[1]
    tn = 1024
    return pl.pallas_call(
        _mod_kernel,
        grid=(n // tn,),
        in_specs=[pl.BlockSpec((r, d), lambda j: (0, 0)),
                  pl.BlockSpec((d, tn), lambda j: (0, j)),
                  pl.BlockSpec((1, tn), lambda j: (0, j))],
        out_specs=pl.BlockSpec((r, tn), lambda j: (0, j)),
        out_shape=jax.ShapeDtypeStruct((r, n), F32),
        compiler_params=_cparams("arbitrary"),
        name="adaln_modulation",
    )(cc, w, b.reshape(1, n))


def _norm_mm_kernel(x_ref, g_ref, sh_ref, sc_ref, w_ref, o_ref, *rest, emit_h):
    h_scr = rest[-1]

    @pl.when(pl.program_id(1) == 0)
    def _():
        x = x_ref[...]
        y = x * lax.rsqrt(jnp.mean(x * x, axis=-1, keepdims=True) + NORM_EPS) * g_ref[...]
        h = y * (1 + sc_ref[...]) + sh_ref[...]
        h_scr[...] = h.astype(BF16)
        if emit_h:
            rest[0][...] = h

    o_ref[...] = jnp.dot(h_scr[...], w_ref[...], preferred_element_type=F32)


def _norm_mm(x, g, shift, scale, row_fn, w, tn, emit_h=False):
    m, d = x.shape
    n = w.shape[1]
    tm = ROW_TILE
    mod_spec = pl.BlockSpec((None, 1, d), lambda i, j: (row_fn(i), 0, 0))
    out_shape = [jax.ShapeDtypeStruct((m, n), F32)]
    out_specs = [pl.BlockSpec((tm, tn), lambda i, j: (i, j))]
    if emit_h:
        out_shape.append(jax.ShapeDtypeStruct((m, d), F32))
        out_specs.append(pl.BlockSpec((tm, d), lambda i, j: (i, 0)))
    res = pl.pallas_call(
        functools.partial(_norm_mm_kernel, emit_h=emit_h),
        grid=(m // tm, n // tn),
        in_specs=[pl.BlockSpec((tm, d), lambda i, j: (i, 0)),
                  pl.BlockSpec((1, d), lambda i, j: (0, 0)),
                  mod_spec, mod_spec,
                  pl.BlockSpec((d, tn), lambda i, j: (0, j))],
        out_specs=out_specs,
        out_shape=out_shape,
        scratch_shapes=[pltpu.VMEM((tm, d), BF16)],
        compiler_params=_cparams("arbitrary", "arbitrary"),
        name="norm_modulate_matmul",
    )(x, g.reshape(1, d), shift, scale, w)
    return res if emit_h else res[0]


def _mm_res_kernel(*refs, n_a):
    a_refs, w_refs = refs[:n_a], refs[n_a:2 * n_a]
    x_ref, gate_ref, o_ref = refs[2 * n_a:]
    acc = jnp.dot(a_refs[0][...], w_refs[0][...], preferred_element_type=F32)
    for a_ref, w_ref in zip(a_refs[1:], w_refs[1:]):
        acc = acc + jnp.dot(a_ref[...], w_ref[...], preferred_element_type=F32)
    o_ref[...] = x_ref[...] + gate_ref[...] * acc


def _mm_residual(a_list, w_list, x, gate, row_fn):
    m, d = x.shape
    tm = ROW_TILE
    n_a = len(a_list)
    in_specs = [pl.BlockSpec((tm, a.shape[1]), lambda i: (i, 0)) for a in a_list]
    in_specs += [pl.BlockSpec(w.shape, lambda i: (0, 0)) for w in w_list]
    in_specs += [pl.BlockSpec((tm, d), lambda i: (i, 0)),
                 pl.BlockSpec((None, 1, d), lambda i: (row_fn(i), 0, 0))]
    return pl.pallas_call(
        functools.partial(_mm_res_kernel, n_a=n_a),
        grid=(m // tm,),
        in_specs=in_specs,
        out_specs=pl.BlockSpec((tm, d), lambda i: (i, 0)),
        out_shape=jax.ShapeDtypeStruct((m, d), F32),
        compiler_params=_cparams("arbitrary"),
        name="matmul_gated_residual",
    )(*a_list, *w_list, x, gate)


def _rms(x, g):
    return x * lax.rsqrt(jnp.mean(x * x, axis=-1, keepdims=True) + NORM_EPS) * g


def _mla_q_kernel(cq_ref, g_ref, w_ref, cos_ref, sin_ref, o_ref, xn_scr):
    @pl.when(pl.program_id(2) == 0)
    def _():
        xn_scr[...] = _rms(cq_ref[...], g_ref[...]).astype(BF16)

    y = jnp.dot(xn_scr[...], w_ref[...], preferred_element_type=F32)
    nope = y[:, :MLA_NOPE_DIM]
    r = y[:, MLA_NOPE_DIM:MLA_QK_DIM]
    rr = y[:, MLA_QK_DIM:]
    o_ref[...] = jnp.concatenate([nope, r * cos_ref[...] + rr * sin_ref[...]], axis=-1).astype(BF16)


def _mla_q(y_in, g, w, cos, sin, nb, sl):
    tm = ROW_TILE
    return pl.pallas_call(
        _mla_q_kernel,
        grid=(nb, sl // tm, MLA_HEADS),
        in_specs=[pl.BlockSpec((tm, MLA_Q_RANK), lambda b, i, h: (b * (sl // tm) + i, 0)),
                  pl.BlockSpec((1, MLA_Q_RANK), lambda b, i, h: (0, 0)),
                  pl.BlockSpec((None, MLA_Q_RANK, 2 * LANES), lambda b, i, h: (h, 0, 0)),
                  pl.BlockSpec((tm, MLA_ROPE_DIM), lambda b, i, h: (i, 0)),
                  pl.BlockSpec((tm, MLA_ROPE_DIM), lambda b, i, h: (i, 0))],
        out_specs=pl.BlockSpec((None, None, tm, MLA_QK_DIM), lambda b, i, h: (b, h, i, 0)),
        out_shape=jax.ShapeDtypeStruct((nb, MLA_HEADS, sl, MLA_QK_DIM), BF16),
        scratch_shapes=[pltpu.VMEM((tm, MLA_Q_RANK), BF16)],
        compiler_params=_cparams("arbitrary", "arbitrary", "arbitrary"),
        name="mla_query_prep",
    )(y_in, g.reshape(1, -1), w, cos, sin)


def _mla_kv_kernel(ckv_ref, kr_ref, g_ref, w_ref, cos_ref, sin_ref, k_ref, v_ref, xn_scr):
    @pl.when(pl.program_id(2) == 0)
    def _():
        xn_scr[...] = _rms(ckv_ref[...], g_ref[...]).astype(BF16)

    y = jnp.dot(xn_scr[...], w_ref[...], preferred_element_type=F32)
    krb = kr_ref[...]
    kr = krb[:, :MLA_ROPE_DIM] * cos_ref[...] + krb[:, MLA_ROPE_DIM:] * sin_ref[...]
    k_ref[...] = jnp.concatenate([y[:, :MLA_NOPE_DIM], kr], axis=-1).astype(BF16)
    v_ref[...] = y[:, MLA_NOPE_DIM:].astype(BF16)


def _mla_kv(y_in, ckv_col, kr_col, g, w, cos, sin, nb, sl):
    tm = ROW_TILE
    return pl.pallas_call(
        _mla_kv_kernel,
        grid=(nb, sl // tm, MLA_HEADS),
        in_specs=[pl.BlockSpec((tm, MLA_KV_RANK), lambda b, i, h: (b * (sl // tm) + i, ckv_col)),
                  pl.BlockSpec((tm, LANES), lambda b, i, h: (b * (sl // tm) + i, kr_col)),
                  pl.BlockSpec((1, MLA_KV_RANK), lambda b, i, h: (0, 0)),
                  pl.BlockSpec((None, MLA_KV_RANK, 2 * LANES), lambda b, i, h: (h, 0, 0)),
                  pl.BlockSpec((tm, MLA_ROPE_DIM), lambda b, i, h: (i, 0)),
                  pl.BlockSpec((tm, MLA_ROPE_DIM), lambda b, i, h: (i, 0))],
        out_specs=[pl.BlockSpec((None, None, tm, MLA_QK_DIM), lambda b, i, h: (b, h, i, 0)),
                   pl.BlockSpec((None, None, tm, MLA_V_DIM), lambda b, i, h: (b, h, i, 0))],
        out_shape=[jax.ShapeDtypeStruct((nb, MLA_HEADS, sl, MLA_QK_DIM), BF16),
                   jax.ShapeDtypeStruct((nb, MLA_HEADS, sl, MLA_V_DIM), BF16)],
        scratch_shapes=[pltpu.VMEM((tm, MLA_KV_RANK), BF16)],
        compiler_params=_cparams("arbitrary", "arbitrary", "arbitrary"),
        name="mla_key_value_prep",
    )(y_in, y_in, g.reshape(1, -1), w, cos, sin)


def _head_kernel(x_ref, g_ref, cos_ref, sin_ref, o_ref, *, norm, rope):
    x = x_ref[...]
    if norm:
        x = _rms(x, g_ref[...])
    if rope:
        q = GQA_HEAD_DIM // 4
        lane = lax.broadcasted_iota(jnp.int32, x.shape, 1)
        first = (lane % (2 * q)) < q
        rot = jnp.where(first, -pltpu.roll(x, GQA_HEAD_DIM - q, 1), pltpu.roll(x, q, 1))
        x = x * cos_ref[...] + rot * sin_ref[...]
    o_ref[...] = x.astype(BF16)


def _head_prep(y_in, col0, n_heads, g, cos, sin, nb, sl, norm, rope):
    tm = ROW_TILE
    return pl.pallas_call(
        functools.partial(_head_kernel, norm=norm, rope=rope),
        grid=(nb, sl // tm, n_heads),
        in_specs=[pl.BlockSpec((tm, GQA_HEAD_DIM), lambda b, i, h: (b * (sl // tm) + i, col0 + h)),
                  pl.BlockSpec((1, GQA_HEAD_DIM), lambda b, i, h: (0, 0)),
                  pl.BlockSpec((tm, GQA_HEAD_DIM), lambda b, i, h: (i, 0)),
                  pl.BlockSpec((tm, GQA_HEAD_DIM), lambda b, i, h: (i, 0))],
        out_specs=pl.BlockSpec((None, None, tm, GQA_HEAD_DIM), lambda b, i, h: (b, h, i, 0)),
        out_shape=jax.ShapeDtypeStruct((nb, n_heads, sl, GQA_HEAD_DIM), BF16),
        compiler_params=_cparams("arbitrary", "arbitrary", "arbitrary"),
        name="gqa_head_prep",
    )(y_in, g.reshape(1, -1), cos, sin)


def _attn_kernel(q_ref, kc_ref, kl_ref, vc_ref, vl_ref, o_ref, *, scale):
    q = q_ref[...]
    dn = (((1,), (1,)), ((), ()))
    sc = lax.dot_general(q, kc_ref[...], dn, preferred_element_type=F32) * scale
    sl = lax.dot_general(q, kl_ref[...], dn, preferred_element_type=F32) * scale
    m = jnp.maximum(jnp.max(sc, axis=-1, keepdims=True), jnp.max(sl, axis=-1, keepdims=True))
    pc = jnp.exp(sc - m)
    pl_ = jnp.exp(sl - m)
    denom = jnp.sum(pc, axis=-1, keepdims=True) + jnp.sum(pl_, axis=-1, keepdims=True)
    o = jnp.dot(pc.astype(BF16), vc_ref[...], preferred_element_type=F32)
    o = o + jnp.dot(pl_.astype(BF16), vl_ref[...], preferred_element_type=F32)
    o_ref[...] = (o / denom).astype(BF16)


def _attention(q, kc, kl, vc, vl, scale):
    nb, nh, sl, dk = q.shape
    hk, tc, tl, dv = kc.shape[1], kc.shape[2], kl.shape[2], vc.shape[3]
    grp = nh // hk
    tq = ATTN_Q_TILE
    kv_spec = lambda t, d: pl.BlockSpec((None, None, t, d), lambda b, h, i: (b, h // grp, 0, 0))
    return pl.pallas_call(
        functools.partial(_attn_kernel, scale=scale),
        grid=(nb, nh, sl // tq),
        in_specs=[pl.BlockSpec((None, None, tq, dk), lambda b, h, i: (b, h, i, 0)),
                  kv_spec(tc, dk), kv_spec(tl, dk), kv_spec(tc, dv), kv_spec(tl, dv)],
        out_specs=pl.BlockSpec((tq, dv), lambda b, h, i: (b * (sl // tq) + i, h)),
        out_shape=jax.ShapeDtypeStruct((nb * sl, nh * dv), BF16),
        compiler_params=_cparams("arbitrary", "arbitrary", "arbitrary"),
        name="softmax_attention",
    )(q, kc, kl, vc, vl)


def _conv_kernel(a_ref, ag_ref, b_ref, c_ref, u_ref,
                 ap_ref, agp_ref, cp_ref, up_ref, an_ref, agn_ref, cn_ref, un_ref,
                 cw_ref, cb_ref, lg_ref, lb_ref, sw_ref, o_ref, ext_scr, ext2_scr):
    i = pl.program_id(1)
    n = pl.num_programs(1)
    ts = a_ref.shape[0]
    hl = CONV_HALO
    has_prev = (i > 0).astype(F32)
    has_next = (i + 1 < n).astype(F32)
    glu = lambda a, g: a * jax.nn.sigmoid(g)
    ext_scr[pl.ds(0, hl), :] = glu(ap_ref[...], agp_ref[...]) * has_prev
    ext_scr[pl.ds(hl, ts), :] = glu(a_ref[...], ag_ref[...])
    ext_scr[pl.ds(hl + ts, hl), :] = glu(an_ref[...], agn_ref[...]) * has_next
    ext2_scr[pl.ds(0, hl), :] = cp_ref[...] * up_ref[...] * has_prev
    ext2_scr[pl.ds(hl, ts), :] = c_ref[...] * u_ref[...]
    ext2_scr[pl.ds(hl + ts, hl), :] = cn_ref[...] * un_ref[...] * has_next

    pad = (CFM_KERNEL - 1) // 2
    acc = cw_ref[pl.ds(0, 1), :] * ext_scr[pl.ds(hl - pad, ts), :]
    for k in range(1, CFM_KERNEL):
        acc = acc + cw_ref[pl.ds(k, 1), :] * ext_scr[pl.ds(hl - pad + k, ts), :]
    acc = acc + cb_ref[...]
    mu = jnp.mean(acc, axis=-1, keepdims=True)
    xc = acc - mu
    yn = xc * lax.rsqrt(jnp.mean(xc * xc, axis=-1, keepdims=True) + NORM_EPS) * lg_ref[...] + lb_ref[...]
    yc = yn * jax.nn.sigmoid(yn)

    pad2 = (SC_KERNEL - 1) // 2
    acc2 = sw_ref[pl.ds(0, 1), :] * ext2_scr[pl.ds(hl - pad2, ts), :]
    for k in range(1, SC_KERNEL):
        acc2 = acc2 + sw_ref[pl.ds(k, 1), :] * ext2_scr[pl.ds(hl - pad2 + k, ts), :]
    yd = b_ref[...] * acc2
    o_ref[...] = jnp.concatenate([yc, yd], axis=-1).astype(BF16)


def _conv_mixer(y_in, cfm_w, cfm_b, ln_g, ln_b, sc_w, nb, sl):
    ts, hl, cw = CONV_TILE, CONV_HALO, CFM_WIDTH
    nt = sl // ts
    rb = ts // hl
    last = nb * sl // hl - 1
    cur = lambda col: pl.BlockSpec((ts, cw), lambda b, i: (b * nt + i, col))
    prev = lambda col: pl.BlockSpec((hl, cw), lambda b, i: (jnp.maximum((b * nt + i) * rb - 1, 0), col))
    nxt = lambda col: pl.BlockSpec((hl, cw), lambda b, i: (jnp.minimum((b * nt + i + 1) * rb, last), col))
    vec = lambda r: pl.BlockSpec((r, cw), lambda b, i: (0, 0))
    return pl.pallas_call(
        _conv_kernel,
        grid=(nb, nt),
        in_specs=[cur(0), cur(1), cur(2), cur(3), cur(4),
                  prev(0), prev(1), prev(3), prev(4), nxt(0), nxt(1), nxt(3), nxt(4),
                  vec(CFM_KERNEL), vec(1), vec(1), vec(1), vec(SC_KERNEL)],
        out_specs=pl.BlockSpec((ts, 2 * cw), lambda b, i: (b * nt + i, 0)),
        out_shape=jax.ShapeDtypeStruct((nb * sl, 2 * cw), BF16),
        scratch_shapes=[pltpu.VMEM((ts + 2 * hl, cw), F32), pltpu.VMEM((ts + 2 * hl, cw), F32)],
        compiler_params=_cparams("arbitrary", "arbitrary"),
        name="conv_mixer",
    )(*([y_in] * 13), cfm_w, cfm_b.reshape(1, cw), ln_g.reshape(1, cw), ln_b.reshape(1, cw), sc_w)


def _top_k_rows(s, payload=None):
    k = s.shape[0]
    iota = lax.broadcasted_iota(jnp.int32, s.shape, 0)
    vals, idxs = [], []
    for _ in range(PEER_TOPK):
        m = jnp.max(s, axis=0, keepdims=True)
        idx = jnp.min(jnp.where(s == m, iota, k), axis=0, keepdims=True)
        hit = iota == idx
        vals.append(m)
        idxs.append(idx if payload is None else jnp.sum(jnp.where(hit, payload, 0), axis=0, keepdims=True))
        s = jnp.where(hit, -jnp.inf, s)
    return jnp.concatenate(vals, axis=0), jnp.concatenate(idxs, axis=0)


def _route_kernel(q_ref, k1_ref, k2_ref, ids_ref, gate_ref):
    half = PEER_KEY_DIM // 2
    q = q_ref[...].astype(BF16)
    dn = (((1,), (1,)), ((), ()))
    s1 = lax.dot_general(k1_ref[...], q[:, :half], dn, preferred_element_type=F32)
    s2 = lax.dot_general(k2_ref[...], q[:, half:], dn, preferred_element_type=F32)
    v1, i1 = _top_k_rows(s1)
    v2, i2 = _top_k_rows(s2)
    cand = jnp.concatenate([v1[a:a + 1, :] + v2 for a in range(PEER_TOPK)], axis=0)
    cand_id = jnp.concatenate([i1[a:a + 1, :] * PEER_N_KEYS + i2 for a in range(PEER_TOPK)], axis=0)
    best, ids = _top_k_rows(cand, cand_id)
    e = jnp.exp(best - jnp.max(best, axis=0, keepdims=True))
    gate_ref[...] = e / jnp.sum(e, axis=0, keepdims=True)
    ids_ref[...] = ids


def _peer_route(q, key1, key2):
    m = q.shape[0]
    tm = ROW_TILE
    out_spec = pl.BlockSpec((PEER_TOPK, tm), lambda i, h: (h, i))
    return pl.pallas_call(
        _route_kernel,
        grid=(m // tm, PEER_HEADS),
        in_specs=[pl.BlockSpec((tm, PEER_KEY_DIM), lambda i, h: (i, h)),
                  pl.BlockSpec(key1.shape, lambda i, h: (0, 0)),
                  pl.BlockSpec(key2.shape, lambda i, h: (0, 0))],
        out_specs=[out_spec, out_spec],
        out_shape=[jax.ShapeDtypeStruct((PEER_PICKS, m), jnp.int32),
                   jax.ShapeDtypeStruct((PEER_PICKS, m), F32)],
        compiler_params=_cparams("arbitrary", "arbitrary"),
        name="peer_route",
    )(q, key1, key2)


def _gather_rows(tt):
    return tt * PEER_PICKS


def _issue_rows(ids_ref, tab_ref, buf_ref, sem, slot, rows):
    groups = rows // SUBLANES

    def body(g, carry):
        base = pl.multiple_of(g * GATHER_UNROLL, GATHER_UNROLL)
        blk = buf_ref.at[pl.ds(slot * groups + g * (GATHER_UNROLL // SUBLANES), GATHER_UNROLL // SUBLANES)]
        for k in range(GATHER_UNROLL):
            pltpu.make_async_copy(tab_ref.at[ids_ref[base + k]],
                                  blk.at[k // SUBLANES, pl.ds(k % SUBLANES, 1)],
                                  sem.at[slot]).start(priority=k % 2)
        return carry

    lax.fori_loop(0, rows // GATHER_UNROLL, body, 0)


def _wait_rows(buf_ref, sem, slot, rows):
    groups = rows // SUBLANES
    half = buf_ref.at[pl.ds(slot * groups, groups)]
    pltpu.make_async_copy(half, half, sem.at[slot]).wait()


def _gather_kernel(ids_cur, ids_nxt, h_ref, gate_ref, x_ref, mg_ref, fg_ref, u_ref, v_ref, o_ref,
                   ubuf, vbuf, usem, vsem, *, final_norm):
    i = pl.program_id(0)
    n = pl.num_programs(0)
    tt = h_ref.shape[0]
    rows = _gather_rows(tt)
    slot = i % 2

    @pl.when(i == 0)
    def _():
        _issue_rows(ids_cur, u_ref, ubuf, usem, 0, rows)
        _issue_rows(ids_cur, v_ref, vbuf, vsem, 0, rows)

    @pl.when(i + 1 < n)
    def _():
        _issue_rows(ids_nxt, u_ref, ubuf, usem, 1 - slot, rows)
        _issue_rows(ids_nxt, v_ref, vbuf, vsem, 1 - slot, rows)

    _wait_rows(ubuf, usem, slot, rows)
    _wait_rows(vbuf, vsem, slot, rows)
    pg = PEER_PICKS // SUBLANES
    d = h_ref.shape[1]
    for t in range(tt):
        ut = ubuf[pl.ds(slot * (rows // SUBLANES) + t * pg, pg)].reshape(PEER_PICKS, d)
        s = jnp.sum(ut * h_ref[pl.ds(t, 1), :], axis=-1, keepdims=True)
        a = jax.nn.gelu(s) * gate_ref[:, pl.ds(t, 1)]
        vt = vbuf[pl.ds(slot * (rows // SUBLANES) + t * pg, pg)].reshape(PEER_PICKS, d)
        y = jnp.sum(vt * a, axis=0, keepdims=True)
        xo = x_ref[pl.ds(t, 1), :] + mg_ref[...] * y
        if final_norm:
            xo = _rms(xo, fg_ref[...])
        o_ref[pl.ds(t, 1), :] = xo


def _peer_mix(h2, ids_t, gate_t, x, mod_gate, row_fn, final_g, u, v, final_norm):
    m, d = h2.shape
    tt = GATHER_TOKENS
    steps = m // tt
    rows = _gather_rows(tt)
    ids = ids_t.T.reshape(m * PEER_PICKS)
    gate = gate_t.reshape(PEER_PICKS, steps, tt).transpose(1, 0, 2)
    ids_spec = lambda f: pl.BlockSpec((rows,), f, memory_space=pltpu.SMEM)
    tok_spec = pl.BlockSpec((tt, d), lambda i: (i, 0))
    buf = pltpu.VMEM((2 * rows // SUBLANES, SUBLANES, d), F32)
    return pl.pallas_call(
        functools.partial(_gather_kernel, final_norm=final_norm),
        grid=(steps,),
        in_specs=[ids_spec(lambda i: (i,)),
                  ids_spec(lambda i: (jnp.minimum(i + 1, steps - 1),)),
                  tok_spec,
                  pl.BlockSpec((None, PEER_PICKS, tt), lambda i: (i, 0, 0)),
                  tok_spec,
                  pl.BlockSpec((None, 1, d), lambda i: (row_fn(i), 0, 0)),
                  pl.BlockSpec((1, d), lambda i: (0, 0)),
                  pl.BlockSpec(memory_space=pl.ANY),
                  pl.BlockSpec(memory_space=pl.ANY)],
        out_specs=tok_spec,
        out_shape=jax.ShapeDtypeStruct((m, d), F32),
        scratch_shapes=[buf, buf, pltpu.SemaphoreType.DMA((2,)), pltpu.SemaphoreType.DMA((2,))],
        compiler_params=_cparams("arbitrary"),
        name="peer_gather_mix",
    )(ids, ids, h2, gate, x, mod_gate, final_g.reshape(1, d), u.reshape(-1, 1, d), v.reshape(-1, 1, d))


def _rope_tables(n_rows, d_rot):
    row = jnp.broadcast_to(jnp.arange(n_rows, dtype=F32)[:, None], (n_rows, GRID_W)).reshape(-1)
    col = jnp.broadcast_to(jnp.arange(GRID_W, dtype=F32)[None, :], (n_rows, GRID_W)).reshape(-1)
    quarter = d_rot // 4
    inv_freq = ROPE_THETA ** (-jnp.arange(quarter, dtype=F32) / quarter)
    ar = row[:, None] * inv_freq
    ac = col[:, None] * inv_freq
    ang = jnp.concatenate([ar, ar, ac, ac], axis=-1)
    return jnp.cos(ang), jnp.sin(ang)


def _rotate_half_cols(w):
    r1, r2, c1, c2 = jnp.split(w, 4, axis=-1)
    return jnp.concatenate([-r2, r1, -c2, c1], axis=-1)


def _peer(x2, mod, tile_row_fn, gather_row_fn, norm_g, w_q, key1, key2, u, v, final_g, final_norm):
    d = x2.shape[1]
    q, h2 = _norm_mm(x2, norm_g, mod[3], mod[4], tile_row_fn, w_q.astype(BF16), tn=d, emit_h=True)
    ids_t, gate_t = _peer_route(q, key1.astype(BF16), key2.astype(BF16))
    return _peer_mix(h2, ids_t, gate_t, x2, mod[5], gather_row_fn, final_g, u, v, final_norm)


def kernel(x, c, ctx, c_ctx, ada_w, ada_b, norm1_g, norm2_g, attn_w_in, mla_q_norm_g, mla_w_uq, mla_kv_norm_g, mla_w_ukv, gqa_q_norm_g, gqa_k_norm_g, attn_w_out, conv_w_in, cfm_dw_w, cfm_dw_b, cfm_ln_g, cfm_ln_b, sc_dw_w, conv_w_out, peer_w_q, peer_key1, peer_key2, peer_u, peer_v, final_norm_g):
    nb, sl, d = x.shape
    tc = ctx.shape[1]
    depth = ada_w.shape[0]
    assert sl % ROW_TILE == 0 and tc % ROW_TILE == 0 and sl % GRID_W == 0 and d == 2048

    x2 = x.reshape(nb * sl, d)
    ctx2 = ctx.reshape(nb * tc, d)
    ctx_row = nb
    cc = jnp.zeros((SUBLANES, d), F32).at[:nb].set(c).at[ctx_row].set(c_ctx)
    lat_row = lambda i: (i * ROW_TILE) // sl
    ctx_row_fn = lambda i: ctx_row
    gat_row = lambda i: (i * GATHER_TOKENS) // sl

    cos_m, sin_m = _rope_tables(sl // GRID_W, MLA_ROPE_DIM)
    cos_g, sin_g = _rope_tables(sl // GRID_W, GQA_HEAD_DIM)
    one = lambda n, w: jnp.ones((n, w), F32)
    zero = lambda n, w: jnp.zeros((n, w), F32)

    for i in range(depth):
        mod = _modulation(cc, ada_w[i], ada_b[i]).reshape(SUBLANES, 6, 1, d).transpose(1, 0, 2, 3)
        if i % 2 == 0:
            a = i // 2
            w_in = attn_w_in[a]
            o_cq, o_ckv = MLA_Q_RANK, MLA_Q_RANK + MLA_KV_RANK
            o_kr = o_ckv + MLA_ROPE_DIM
            o_gk = o_kr + GQA_HEADS * GQA_HEAD_DIM
            o_gv = o_gk + GQA_KV_HEADS * GQA_HEAD_DIM
            w_kr = w_in[:, o_ckv:o_kr]
            w_in_r = jnp.concatenate([w_in[:, :o_ckv], w_in[:, o_kr:], w_kr, _rotate_half_cols(w_kr)], axis=1).astype(BF16)
            c_ckv = MLA_Q_RANK // MLA_KV_RANK
            c_gq = o_ckv // LANES
            c_gk = c_gq + GQA_HEADS
            c_gv = c_gk + GQA_KV_HEADS
            c_kr = c_gv + GQA_KV_HEADS
            wq = mla_w_uq[a].reshape(MLA_Q_RANK, MLA_HEADS, MLA_QK_DIM)
            wq = jnp.concatenate([wq, _rotate_half_cols(wq[..., MLA_NOPE_DIM:])], axis=-1).transpose(1, 0, 2).astype(BF16)
            wkv = mla_w_ukv[a].reshape(MLA_KV_RANK, MLA_HEADS, MLA_NOPE_DIM + MLA_V_DIM).transpose(1, 0, 2).astype(BF16)

            y_l = _norm_mm(x2, norm1_g[i], mod[0], mod[1], lat_row, w_in_r, tn=w_in_r.shape[1])
            y_c = _norm_mm(ctx2, norm1_g[i], mod[0], mod[1], ctx_row_fn, w_in_r, tn=w_in_r.shape[1])

            mq = _mla_q(y_l, mla_q_norm_g[a], wq, cos_m, sin_m, nb, sl)
            mk, mv = _mla_kv(y_l, c_ckv, c_kr, mla_kv_norm_g[a], wkv, cos_m, sin_m, nb, sl)
            mkc, mvc = _mla_kv(y_c, c_ckv, c_kr, mla_kv_norm_g[a], wkv, one(tc, MLA_ROPE_DIM), zero(tc, MLA_ROPE_DIM), nb, tc)
            hd = GQA_HEAD_DIM
            gq = _head_prep(y_l, c_gq, GQA_HEADS, gqa_q_norm_g[a], cos_g, sin_g, nb, sl, True, True)
            gk = _head_prep(y_l, c_gk, GQA_KV_HEADS, gqa_k_norm_g[a], cos_g, sin_g, nb, sl, True, True)
            gv = _head_prep(y_l, c_gv, GQA_KV_HEADS, gqa_k_norm_g[a], cos_g, sin_g, nb, sl, False, False)
            gkc = _head_prep(y_c, c_gk, GQA_KV_HEADS, gqa_k_norm_g[a], one(tc, hd), zero(tc, hd), nb, tc, True, False)
            gvc = _head_prep(y_c, c_gv, GQA_KV_HEADS, gqa_k_norm_g[a], one(tc, hd), zero(tc, hd), nb, tc, False, False)

            o_m = _attention(mq, mkc, mk, mvc, mv, MLA_QK_DIM ** -0.5)
            o_g = _attention(gq, gkc, gk, gvc, gv, GQA_HEAD_DIM ** -0.5)
            w_out = attn_w_out[a].astype(BF16)
            n_m = MLA_HEADS * MLA_V_DIM
            x2 = _mm_residual([o_m, o_g], [w_out[:n_m], w_out[n_m:]], x2, mod[2], lat_row)
        else:
            mi = i // 2
            y = _norm_mm(x2, norm1_g[i], mod[0], mod[1], lat_row, conv_w_in[mi].astype(BF16), tn=1280)
            yc = _conv_mixer(y, cfm_dw_w[mi], cfm_dw_b[mi], cfm_ln_g[mi], cfm_ln_b[mi], sc_dw_w[mi], nb, sl)
            x2 = _mm_residual([yc], [conv_w_out[mi].astype(BF16)], x2, mod[2], lat_row)
        x2 = _peer(x2, mod, lat_row, gat_row, norm2_g[i], peer_w_q[i], peer_key1[i], peer_key2[i], peer_u[i], peer_v[i],
                   final_norm_g, final_norm=(i == depth - 1))
    return x2.reshape(nb, sl, d)
```

```python
import functools

import jax
import jax.numpy as jnp
from jax import lax
from jax.experimental import pallas as pl
from jax.experimental.pallas import tpu as pltpu

F32 = jnp.float32
BF16 = jnp.bfloat16

GRID_W = 64
ROPE_THETA = 10000.0
NORM_EPS = 1e-6
MLA_HEADS = 8
MLA_Q_RANK = 512
MLA_KV_RANK = 256
MLA_NOPE_DIM = 128
MLA_ROPE_DIM = 64
MLA_V_DIM = 128
MLA_QK_DIM = MLA_NOPE_DIM + MLA_ROPE_DIM
GQA_HEADS = 8
GQA_KV_HEADS = 2
GQA_HEAD_DIM = 128
CFM_WIDTH = 1024
CFM_KERNEL = 31
SC_WIDTH = 1024
SC_KERNEL = 3
PEER_HEADS = 8
PEER_KEY_DIM = 256
PEER_N_KEYS = 128
PEER_TOPK = 16
PEER_PICKS = PEER_HEADS * PEER_TOPK

LANES = 128
SUBLANES = 8
VMEM_LIMIT = 48 * 1024 * 1024

ROW_TILE = 256
ATTN_Q_TILE = 256
CONV_TILE = 256
CONV_HALO = 16
GATHER_TOKENS = 8


def _cparams(*sem):
    return pltpu.CompilerParams(dimension_semantics=sem, vmem_limit_bytes=VMEM_LIMIT)


def _mod_kernel(c_ref, w_ref, b_ref, o_ref):
    c = c_ref[...]
    a = (c * jax.nn.sigmoid(c)).astype(BF16)
    o_ref[...] = jnp.dot(a, w_ref[...].astype(BF16), preferred_element_type=F32) + b_ref[...]


def _modulation(cc, w, b):
    r, d = cc.shape
    n = w.shape[1]
    tn = 1024
    return pl.pallas_call(
        _mod_kernel,
        grid=(n // tn,),
        in_specs=[pl.BlockSpec((r, d), lambda j: (0, 0)),
                  pl.BlockSpec((d, tn), lambda j: (0, j)),
                  pl.BlockSpec((1, tn), lambda j: (0, j))],
        out_specs=pl.BlockSpec((r, tn), lambda j: (0, j)),
        out_shape=jax.ShapeDtypeStruct((r, n), F32),
        compiler_params=_cparams("arbitrary"),
        name="adaln_modulation",
    )(cc, w, b.reshape(1, n))


def _norm_mm_kernel(x_ref, g_ref, sh_ref, sc_ref, w_ref, o_ref, *rest, emit_h):
    h_scr = rest[-1]

    @pl.when(pl.program_id(1) == 0)
    def _():
        x = x_ref[...]
        y = x * lax.rsqrt(jnp.mean(x * x, axis=-1, keepdims=True) + NORM_EPS) * g_ref[...]
        h = y * (1 + sc_ref[...]) + sh_ref[...]
        h_scr[...] = h.astype(BF16)
        if emit_h:
            rest[0][...] = h

    o_ref[...] = jnp.dot(h_scr[...], w_ref[...], preferred_element_type=F32)


def _norm_mm(x, g, shift, scale, row_fn, w, tn, emit_h=False):
    m, d = x.shape
    n = w.shape[1]
    tm = ROW_TILE
    mod_spec = pl.BlockSpec((None, 1, d), lambda i, j: (row_fn(i), 0, 0))
    out_shape = [jax.ShapeDtypeStruct((m, n), F32)]
    out_specs = [pl.BlockSpec((tm, tn), lambda i, j: (i, j))]
    if emit_h:
        out_shape.append(jax.ShapeDtypeStruct((m, d), F32))
        out_specs.append(pl.BlockSpec((tm, d), lambda i, j: (i, 0)))
    res = pl.pallas_call(
        functools.partial(_norm_mm_kernel, emit_h=emit_h),
        grid=(m // tm, n // tn),
        in_specs=[pl.BlockSpec((tm, d), lambda i, j: (i, 0)),
                  pl.BlockSpec((1, d), lambda i, j: (0, 0)),
                  mod_spec, mod_spec,
                  pl.BlockSpec((d, tn), lambda i, j: (0, j))],
        out_specs=out_specs,
        out_shape=out_shape,
        scratch_shapes=[pltpu.VMEM((tm, d), BF16)],
        compiler_params=_cparams("arbitrary", "arbitrary"),
        name="norm_modulate_matmul",
    )(x, g.reshape(1, d), shift, scale, w)
    return res if emit_h else res[0]


def _mm_res_kernel(*refs, n_a):
    a_refs, w_refs = refs[:n_a], refs[n_a:2 * n_a]
    x_ref, gate_ref, o_ref = refs[2 * n_a:]
    acc = jnp.dot(a_refs[0][...], w_refs[0][...], preferred_element_type=F32)
    for a_ref, w_ref in zip(a_refs[1:], w_refs[1:]):
        acc = acc + jnp.dot(a_ref[...], w_ref[...], preferred_element_type=F32)
    o_ref[...] = x_ref[...] + gate_ref[...] * acc


def _mm_residual(a_list, w_list, x, gate, row_fn):
    m, d = x.shape
    tm = ROW_TILE
    n_a = len(a_list)
    in_specs = [pl.BlockSpec((tm, a.shape[1]), lambda i: (i, 0)) for a in a_list]
    in_specs += [pl.BlockSpec(w.shape, lambda i: (0, 0)) for w in w_list]
    in_specs += [pl.BlockSpec((tm, d), lambda i: (i, 0)),
                 pl.BlockSpec((None, 1, d), lambda i: (row_fn(i), 0, 0))]
    return pl.pallas_call(
        functools.partial(_mm_res_kernel, n_a=n_a),
        grid=(m // tm,),
        in_specs=in_specs,
        out_specs=pl.BlockSpec((tm, d), lambda i: (i, 0)),
        out_shape=jax.ShapeDtypeStruct((m, d), F32),
        compiler_params=_cparams("arbitrary"),
        name="matmul_gated_residual",
    )(*a_list, *w_list, x, gate)


def _rms(x, g):
    return x * lax.rsqrt(jnp.mean(x * x, axis=-1, keepdims=True) + NORM_EPS) * g


def _mla_q_kernel(cq_ref, g_ref, w_ref, cos_ref, sin_ref, o_ref, xn_scr):
    @pl.when(pl.program_id(2) == 0)
    def _():
        xn_scr[...] = _rms(cq_ref[...], g_ref[...]).astype(BF16)

    y = jnp.dot(xn_scr[...], w_ref[...], preferred_element_type=F32)
    nope = y[:, :MLA_NOPE_DIM]
    r = y[:, MLA_NOPE_DIM:MLA_QK_DIM]
    rr = y[:, MLA_QK_DIM:]
    o_ref[...] = jnp.concatenate([nope, r * cos_ref[...] + rr * sin_ref[...]], axis=-1).astype(BF16)


def _mla_q(y_in, g, w, cos, sin, nb, sl):
    tm = ROW_TILE
    return pl.pallas_call(
        _mla_q_kernel,
        grid=(nb, sl // tm, MLA_HEADS),
        in_specs=[pl.BlockSpec((tm, MLA_Q_RANK), lambda b, i, h: (b * (sl // tm) + i, 0)),
                  pl.BlockSpec((1, MLA_Q_RANK), lambda b, i, h: (0, 0)),
                  pl.BlockSpec((None, MLA_Q_RANK, 2 * LANES), lambda b, i, h: (h, 0, 0)),
                  pl.BlockSpec((tm, MLA_ROPE_DIM), lambda b, i, h: (i, 0)),
                  pl.BlockSpec((tm, MLA_ROPE_DIM), lambda b, i, h: (i, 0))],
        out_specs=pl.BlockSpec((None, None, tm, MLA_QK_DIM), lambda b, i, h: (b, h, i, 0)),
        out_shape=jax.ShapeDtypeStruct((nb, MLA_HEADS, sl, MLA_QK_DIM), BF16),
        scratch_shapes=[pltpu.VMEM((tm, MLA_Q_RANK), BF16)],
        compiler_params=_cparams("arbitrary", "arbitrary", "arbitrary"),
        name="mla_query_prep",
    )(y_in, g.reshape(1, -1), w, cos, sin)


def _mla_kv_kernel(ckv_ref, kr_ref, g_ref, w_ref, cos_ref, sin_ref, k_ref, v_ref, xn_scr):
    @pl.when(pl.program_id(2) == 0)
    def _():
        xn_scr[...] = _rms(ckv_ref[...], g_ref[...]).astype(BF16)

    y = jnp.dot(xn_scr[...], w_ref[...], preferred_element_type=F32)
    krb = kr_ref[...]
    kr = krb[:, :MLA_ROPE_DIM] * cos_ref[...] + krb[:, MLA_ROPE_DIM:] * sin_ref[...]
    k_ref[...] = jnp.concatenate([y[:, :MLA_NOPE_DIM], kr], axis=-1).astype(BF16)
    v_ref[...] = y[:, MLA_NOPE_DIM:].astype(BF16)


def _mla_kv(y_in, ckv_col, kr_col, g, w, cos, sin, nb, sl):
    tm = ROW_TILE
    return pl.pallas_call(
        _mla_kv_kernel,
        grid=(nb, sl // tm, MLA_HEADS),
        in_specs=[pl.BlockSpec((tm, MLA_KV_RANK), lambda b, i, h: (b * (sl // tm) + i, ckv_col)),
                  pl.BlockSpec((tm, LANES), lambda b, i, h: (b * (sl // tm) + i, kr_col)),
                  pl.BlockSpec((1, MLA_KV_RANK), lambda b, i, h: (0, 0)),
                  pl.BlockSpec((None, MLA_KV_RANK, 2 * LANES), lambda b, i, h: (h, 0, 0)),
                  pl.BlockSpec((tm, MLA_ROPE_DIM), lambda b, i, h: (i, 0)),
                  pl.BlockSpec((tm, MLA_ROPE_DIM), lambda b, i, h: (i, 0))],
        out_specs=[pl.BlockSpec((None, None, tm, MLA_QK_DIM), lambda b, i, h: (b, h, i, 0)),
                   pl.BlockSpec((None, None, tm, MLA_V_DIM), lambda b, i, h: (b, h, i, 0))],
        out_shape=[jax.ShapeDtypeStruct((nb, MLA_HEADS, sl, MLA_QK_DIM), BF16),
                   jax.ShapeDtypeStruct((nb, MLA_HEADS, sl, MLA_V_DIM), BF16)],
        scratch_shapes=[pltpu.VMEM((tm, MLA_KV_RANK), BF16)],
        compiler_params=_cparams("arbitrary", "arbitrary", "arbitrary"),
        name="mla_key_value_prep",
    )(y_in, y_in, g.reshape(1, -1), w, cos, sin)


def _head_kernel(x_ref, g_ref, cos_ref, sin_ref, o_ref, *, norm, rope):
    x = x_ref[...]
    if norm:
        x = _rms(x, g_ref[...])
    if rope:
        q = GQA_HEAD_DIM // 4
        lane = lax.broadcasted_iota(jnp.int32, x.shape, 1)
        first = (lane % (2 * q)) < q
        rot = jnp.where(first, -pltpu.roll(x, GQA_HEAD_DIM - q, 1), pltpu.roll(x, q, 1))
        x = x * cos_ref[...] + rot * sin_ref[...]
    o_ref[...] = x.astype(BF16)


def _head_prep(y_in, col0, n_heads, g, cos, sin, nb, sl, norm, rope):
    tm = ROW_TILE
    return pl.pallas_call(
        functools.partial(_head_kernel, norm=norm, rope=rope),
        grid=(nb, sl // tm, n_heads),
        in_specs=[pl.BlockSpec((tm, GQA_HEAD_DIM), lambda b, i, h: (b * (sl // tm) + i, col0 + h)),
                  pl.BlockSpec((1, GQA_HEAD_DIM), lambda b, i, h: (0, 0)),
                  pl.BlockSpec((tm, GQA_HEAD_DIM), lambda b, i, h: (i, 0)),
                  pl.BlockSpec((tm, GQA_HEAD_DIM), lambda b, i, h: (i, 0))],
        out_specs=pl.BlockSpec((None, None, tm, GQA_HEAD_DIM), lambda b, i, h: (b, h, i, 0)),
        out_shape=jax.ShapeDtypeStruct((nb, n_heads, sl, GQA_HEAD_DIM), BF16),
        compiler_params=_cparams("arbitrary", "arbitrary", "arbitrary"),
        name="gqa_head_prep",
    )(y_in, g.reshape(1, -1), cos, sin)


def _attn_kernel(q_ref, kc_ref, kl_ref, vc_ref, vl_ref, o_ref, *, scale):
    q = q_ref[...]
    dn = (((1,), (1,)), ((), ()))
    sc = lax.dot_general(q, kc_ref[...], dn, preferred_element_type=F32) * scale
    sl = lax.dot_general(q, kl_ref[...], dn, preferred_element_type=F32) * scale
    m = jnp.maximum(jnp.max(sc, axis=-1, keepdims=True), jnp.max(sl, axis=-1, keepdims=True))
    pc = jnp.exp(sc - m)
    pl_ = jnp.exp(sl - m)
    denom = jnp.sum(pc, axis=-1, keepdims=True) + jnp.sum(pl_, axis=-1, keepdims=True)
    o = jnp.dot(pc.astype(BF16), vc_ref[...], preferred_element_type=F32)
    o = o + jnp.dot(pl_.astype(BF16), vl_ref[...], preferred_element_type=F32)
    o_ref[...] = (o / denom).astype(BF16)


def _attention(q, kc, kl, vc, vl, scale):
    nb, nh, sl, dk = q.shape
    hk, tc, tl, dv = kc.shape[1], kc.shape[2], kl.shape[2], vc.shape[3]
    grp = nh // hk
    tq = ATTN_Q_TILE
    kv_spec = lambda t, d: pl.BlockSpec((None, None, t, d), lambda b, h, i: (b, h // grp, 0, 0))
    return pl.pallas_call(
        functools.partial(_attn_kernel, scale=scale),
        grid=(nb, nh, sl // tq),
        in_specs=[pl.BlockSpec((None, None, tq, dk), lambda b, h, i: (b, h, i, 0)),
                  kv_spec(tc, dk), kv_spec(tl, dk), kv_spec(tc, dv), kv_spec(tl, dv)],
        out_specs=pl.BlockSpec((tq, dv), lambda b, h, i: (b * (sl // tq) + i, h)),
        out_shape=jax.ShapeDtypeStruct((nb * sl, nh * dv), BF16),
        compiler_params=_cparams("arbitrary", "arbitrary", "arbitrary"),
        name="softmax_attention",
    )(q, kc, kl, vc, vl)


def _conv_kernel(a_ref, ag_ref, b_ref, c_ref, u_ref,
                 ap_ref, agp_ref, cp_ref, up_ref, an_ref, agn_ref, cn_ref, un_ref,
                 cw_ref, cb_ref, lg_ref, lb_ref, sw_ref, o_ref, ext_scr, ext2_scr):
    i = pl.program_id(1)
    n = pl.num_programs(1)
    ts = a_ref.shape[0]
    hl = CONV_HALO
    has_prev = (i > 0).astype(F32)
    has_next = (i + 1 < n).astype(F32)
    glu = lambda a, g: a * jax.nn.sigmoid(g)
    ext_scr[pl.ds(0, hl), :] = glu(ap_ref[...], agp_ref[...]) * has_prev
    ext_scr[pl.ds(hl, ts), :] = glu(a_ref[...], ag_ref[...])
    ext_scr[pl.ds(hl + ts, hl), :] = glu(an_ref[...], agn_ref[...]) * has_next
    ext2_scr[pl.ds(0, hl), :] = cp_ref[...] * up_ref[...] * has_prev
    ext2_scr[pl.ds(hl, ts), :] = c_ref[...] * u_ref[...]
    ext2_scr[pl.ds(hl + ts, hl), :] = cn_ref[...] * un_ref[...] * has_next

    pad = (CFM_KERNEL - 1) // 2
    acc = cw_ref[pl.ds(0, 1), :] * ext_scr[pl.ds(hl - pad, ts), :]
    for k in range(1, CFM_KERNEL):
        acc = acc + cw_ref[pl.ds(k, 1), :] * ext_scr[pl.ds(hl - pad + k, ts), :]
    acc = acc + cb_ref[...]
    mu = jnp.mean(acc, axis=-1, keepdims=True)
    xc = acc - mu
    yn = xc * lax.rsqrt(jnp.mean(xc * xc, axis=-1, keepdims=True) + NORM_EPS) * lg_ref[...] + lb_ref[...]
    yc = yn * jax.nn.sigmoid(yn)

    pad2 = (SC_KERNEL - 1) // 2
    acc2 = sw_ref[pl.ds(0, 1), :] * ext2_scr[pl.ds(hl - pad2, ts), :]
    for k in range(1, SC_KERNEL):
        acc2 = acc2 + sw_ref[pl.ds(k, 1), :] * ext2_scr[pl.ds(hl - pad2 + k, ts), :]
    yd = b_ref[...] * acc2
    o_ref[...] = jnp.concatenate([yc, yd], axis=-1).astype(BF16)


def _conv_mixer(y_in, cfm_w, cfm_b, ln_g, ln_b, sc_w, nb, sl):
    ts, hl, cw = CONV_TILE, CONV_HALO, CFM_WIDTH
    nt = sl // ts
    rb = ts // hl
    last = nb * sl // hl - 1
    cur = lambda col: pl.BlockSpec((ts, cw), lambda b, i: (b * nt + i, col))
    prev = lambda col: pl.BlockSpec((hl, cw), lambda b, i: (jnp.maximum((b * nt + i) * rb - 1, 0), col))
    nxt = lambda col: pl.BlockSpec((hl, cw), lambda b, i: (jnp.minimum((b * nt + i + 1) * rb, last), col))
    vec = lambda r: pl.BlockSpec((r, cw), lambda b, i: (0, 0))
    return pl.pallas_call(
        _conv_kernel,
        grid=(nb, nt),
        in_specs=[cur(0), cur(1), cur(2), cur(3), cur(4),
                  prev(0), prev(1), prev(3), prev(4), nxt(0), nxt(1), nxt(3), nxt(4),
                  vec(CFM_KERNEL), vec(1), vec(1), vec(1), vec(SC_KERNEL)],
        out_specs=pl.BlockSpec((ts, 2 * cw), lambda b, i: (b * nt + i, 0)),
        out_shape=jax.ShapeDtypeStruct((nb * sl, 2 * cw), BF16),
        scratch_shapes=[pltpu.VMEM((ts + 2 * hl, cw), F32), pltpu.VMEM((ts + 2 * hl, cw), F32)],
        compiler_params=_cparams("arbitrary", "arbitrary"),
        name="conv_mixer",
    )(*([y_in] * 13), cfm_w, cfm_b.reshape(1, cw), ln_g.reshape(1, cw), ln_b.reshape(1, cw), sc_w)


_NO_RANK = 1 << 20


def _top_k_rows(s, rank=None, payload=None):
    if rank is None:
        rank = lax.broadcasted_iota(jnp.int32, s.shape, 0)
    vals, idxs = [], []
    for _ in range(PEER_TOPK):
        m = jnp.max(s, axis=0, keepdims=True)
        idx = jnp.min(jnp.where(s == m, rank, _NO_RANK), axis=0, keepdims=True)
        hit = rank == idx
        vals.append(m)
        idxs.append(idx if payload is None else jnp.sum(jnp.where(hit, payload, 0), axis=0, keepdims=True))
        s = jnp.where(hit, -jnp.inf, s)
    return jnp.concatenate(vals, axis=0), jnp.concatenate(idxs, axis=0)


def _pair_candidates(v1, i1, v2, i2):
    k = PEER_TOPK
    r = lax.broadcasted_iota(jnp.int32, (SUBLANES, v1.shape[1]), 0)
    lo, hi = slice(0, SUBLANES), slice(SUBLANES, 2 * SUBLANES)
    groups = []

    def row_a(a, b_rows, n_valid):
        b0 = b_rows.start
        groups.append((v1[a:a + 1] + v2[b_rows], i1[a:a + 1] * PEER_N_KEYS + i2[b_rows], a * k + b0 + r,
                       None if n_valid == SUBLANES else r < n_valid))

    def col_b(b, a_rows, valid):
        a0 = a_rows.start
        groups.append((v1[a_rows] + v2[b:b + 1], i1[a_rows] * PEER_N_KEYS + i2[b:b + 1], (a0 + r) * k + b, valid))

    row_a(0, lo, 8)
    row_a(0, hi, 8)
    row_a(1, lo, 8)
    col_b(0, hi, None)
    row_a(2, lo, k // 3)
    row_a(3, lo, k // 4)
    row_a(4, lo, k // 5)
    col_b(0, lo, r >= 5)
    col_b(1, lo, r >= 5)
    vals = jnp.concatenate([v if ok is None else jnp.where(ok, v, -jnp.inf) for v, _, _, ok in groups], axis=0)
    flat = jnp.concatenate([f if ok is None else jnp.where(ok, f, _NO_RANK) for _, _, f, ok in groups], axis=0)
    ids = jnp.concatenate([c if ok is None else jnp.where(ok, c, 0) for _, c, _, ok in groups], axis=0)
    return vals, flat, ids


def _route_kernel(q_ref, k1_ref, k2_ref, ids_ref, gate_ref):
    half = PEER_KEY_DIM // 2
    q = q_ref[...].astype(BF16)
    dn = (((1,), (1,)), ((), ()))
    s1 = lax.dot_general(k1_ref[...], q[:, :half], dn, preferred_element_type=F32)
    s2 = lax.dot_general(k2_ref[...], q[:, half:], dn, preferred_element_type=F32)
    v1, i1 = _top_k_rows(s1)
    v2, i2 = _top_k_rows(s2)
    cand, flat, cand_id = _pair_candidates(v1, i1, v2, i2)
    best, ids = _top_k_rows(cand, flat, cand_id)
    e = jnp.exp(best - jnp.max(best, axis=0, keepdims=True))
    gate_ref[...] = e / jnp.sum(e, axis=0, keepdims=True)
    ids_ref[...] = ids


def _peer_route(q, key1, key2):
    m = q.shape[0]
    tm = ROW_TILE
    out_spec = pl.BlockSpec((PEER_TOPK, tm), lambda i, h: (h, i))
    return pl.pallas_call(
        _route_kernel,
        grid=(m // tm, PEER_HEADS),
        in_specs=[pl.BlockSpec((tm, PEER_KEY_DIM), lambda i, h: (i, h)),
                  pl.BlockSpec(key1.shape, lambda i, h: (0, 0)),
                  pl.BlockSpec(key2.shape, lambda i, h: (0, 0))],
        out_specs=[out_spec, out_spec],
        out_shape=[jax.ShapeDtypeStruct((PEER_PICKS, m), jnp.int32),
                   jax.ShapeDtypeStruct((PEER_PICKS, m), F32)],
        compiler_params=_cparams("arbitrary", "arbitrary"),
        name="peer_route",
    )(q, key1, key2)


def _gather_rows(tt):
    return tt * PEER_PICKS


def _issue_rows(ids_ref, tabs, bufs, sems, slot, rows, row0, n_rows, active):
    groups = rows // SUBLANES

    def body(g, carry):
        base = pl.multiple_of(row0 + g * SUBLANES, SUBLANES)
        grp = slot * groups + row0 // SUBLANES + g
        for j, (tab_ref, buf_ref, sem) in enumerate(zip(tabs, bufs, sems)):
            blk = buf_ref.at[grp]
            for k in range(SUBLANES):
                pltpu.make_async_copy(tab_ref.at[ids_ref[base + k]], blk.at[pl.ds(k, 1)],
                                      sem.at[slot]).start(priority=(j + k) % 2)
        return carry

    lax.fori_loop(0, active * (n_rows // SUBLANES), body, 0)


def _wait_rows(buf_ref, sem, slot, rows):
    groups = rows // SUBLANES
    half = buf_ref.at[pl.ds(slot * groups, groups)]
    pltpu.make_async_copy(half, half, sem.at[slot]).wait()


def _gather_kernel(ids_cur, ids_nxt, h_ref, gate_ref, x_ref, mg_ref, fg_ref, u_ref, v_ref, o_ref,
                   ubuf, vbuf, usem, vsem, *, final_norm):
    i = pl.program_id(0)
    n = pl.num_programs(0)
    tt = h_ref.shape[0]
    rows = _gather_rows(tt)
    slot = i % 2
    tabs, bufs, sems = (u_ref, v_ref), (ubuf, vbuf), (usem, vsem)

    @pl.when(i == 0)
    def _():
        _issue_rows(ids_cur, tabs, bufs, sems, 0, rows, 0, rows, 1)

    _wait_rows(ubuf, usem, slot, rows)
    _wait_rows(vbuf, vsem, slot, rows)
    has_next = (i + 1 < n).astype(jnp.int32)
    pg = PEER_PICKS // SUBLANES
    d = h_ref.shape[1]
    for t in range(tt):
        _issue_rows(ids_nxt, tabs, bufs, sems, 1 - slot, rows, t * PEER_PICKS, PEER_PICKS, has_next)
        ut = ubuf[pl.ds(slot * (rows // SUBLANES) + t * pg, pg)].reshape(PEER_PICKS, d)
        s = jnp.sum(ut * h_ref[pl.ds(t, 1), :], axis=-1, keepdims=True)
        a = jax.nn.gelu(s) * gate_ref[:, pl.ds(t, 1)]
        vt = vbuf[pl.ds(slot * (rows // SUBLANES) + t * pg, pg)].reshape(PEER_PICKS, d)
        y = jnp.sum(vt * a, axis=0, keepdims=True)
        o_ref[pl.ds(t, 1), :] = x_ref[pl.ds(t, 1), :] + mg_ref[...] * y
    if final_norm:
        o_ref[...] = _rms(o_ref[...], fg_ref[...])


def _peer_mix(h2, ids_t, gate_t, x, mod_gate, row_fn, final_g, u, v, final_norm):
    m, d = h2.shape
    tt = GATHER_TOKENS
    steps = m // tt
    rows = _gather_rows(tt)
    ids = ids_t.T.reshape(m * PEER_PICKS)
    gate = gate_t.reshape(PEER_PICKS, steps, tt).transpose(1, 0, 2)
    ids_spec = lambda f: pl.BlockSpec((rows,), f, memory_space=pltpu.SMEM)
    tok_spec = pl.BlockSpec((tt, d), lambda i: (i, 0))
    buf = pltpu.VMEM((2 * rows // SUBLANES, SUBLANES, d), F32)
    return pl.pallas_call(
        functools.partial(_gather_kernel, final_norm=final_norm),
        grid=(steps,),
        in_specs=[ids_spec(lambda i: (i,)),
                  ids_spec(lambda i: (jnp.minimum(i + 1, steps - 1),)),
                  tok_spec,
                  pl.BlockSpec((None, PEER_PICKS, tt), lambda i: (i, 0, 0)),
                  tok_spec,
                  pl.BlockSpec((None, 1, d), lambda i: (row_fn(i), 0, 0)),
                  pl.BlockSpec((1, d), lambda i: (0, 0)),
                  pl.BlockSpec(memory_space=pl.ANY),
                  pl.BlockSpec(memory_space=pl.ANY)],
        out_specs=tok_spec,
        out_shape=jax.ShapeDtypeStruct((m, d), F32),
        scratch_shapes=[buf, buf, pltpu.SemaphoreType.DMA((2,)), pltpu.SemaphoreType.DMA((2,))],
        compiler_params=_cparams("arbitrary"),
        name="peer_gather_mix",
    )(ids, ids, h2, gate, x, mod_gate, final_g.reshape(1, d), u.reshape(-1, 1, d), v.reshape(-1, 1, d))


def _rope_tables(n_rows, d_rot):
    row = jnp.broadcast_to(jnp.arange(n_rows, dtype=F32)[:, None], (n_rows, GRID_W)).reshape(-1)
    col = jnp.broadcast_to(jnp.arange(GRID_W, dtype=F32)[None, :], (n_rows, GRID_W)).reshape(-1)
    quarter = d_rot // 4
    inv_freq = ROPE_THETA ** (-jnp.arange(quarter, dtype=F32) / quarter)
    ar = row[:, None] * inv_freq
    ac = col[:, None] * inv_freq
    ang = jnp.concatenate([ar, ar, ac, ac], axis=-1)
    return jnp.cos(ang), jnp.sin(ang)


def _rotate_half_cols(w):
    r1, r2, c1, c2 = jnp.split(w, 4, axis=-1)
    return jnp.concatenate([-r2, r1, -c2, c1], axis=-1)


def _peer(x2, mod, tile_row_fn, gather_row_fn, norm_g, w_q, key1, key2, u, v, final_g, final_norm):
    d = x2.shape[1]
    q, h2 = _norm_mm(x2, norm_g, mod[3], mod[4], tile_row_fn, w_q.astype(BF16), tn=d, emit_h=True)
    ids_t, gate_t = _peer_route(q, key1.astype(BF16), key2.astype(BF16))
    return _peer_mix(h2, ids_t, gate_t, x2, mod[5], gather_row_fn, final_g, u, v, final_norm)


def kernel(x, c, ctx, c_ctx, ada_w, ada_b, norm1_g, norm2_g, attn_w_in, mla_q_norm_g, mla_w_uq, mla_kv_norm_g, mla_w_ukv, gqa_q_norm_g, gqa_k_norm_g, attn_w_out, conv_w_in, cfm_dw_w, cfm_dw_b, cfm_ln_g, cfm_ln_b, sc_dw_w, conv_w_out, peer_w_q, peer_key1, peer_key2, peer_u, peer_v, final_norm_g):
    nb, sl, d = x.shape
    tc = ctx.shape[1]
    depth = ada_w.shape[0]
    assert sl % ROW_TILE == 0 and tc % ROW_TILE == 0 and sl % GRID_W == 0 and d == 2048

    x2 = x.reshape(nb * sl, d)
    ctx2 = ctx.reshape(nb * tc, d)
    ctx_row = nb
    cc = jnp.zeros((SUBLANES, d), F32).at[:nb].set(c).at[ctx_row].set(c_ctx)
    lat_row = lambda i: (i * ROW_TILE) // sl
    ctx_row_fn = lambda i: ctx_row
    gat_row = lambda i: (i * GATHER_TOKENS) // sl

    cos_m, sin_m = _rope_tables(sl // GRID_W, MLA_ROPE_DIM)
    cos_g, sin_g = _rope_tables(sl // GRID_W, GQA_HEAD_DIM)
    one = lambda n, w: jnp.ones((n, w), F32)
    zero = lambda n, w: jnp.zeros((n, w), F32)

    for i in range(depth):
        mod = _modulation(cc, ada_w[i], ada_b[i]).reshape(SUBLANES, 6, 1, d).transpose(1, 0, 2, 3)
        if i % 2 == 0:
            a = i // 2
            w_in = attn_w_in[a]
            o_cq, o_ckv = MLA_Q_RANK, MLA_Q_RANK + MLA_KV_RANK
            o_kr = o_ckv + MLA_ROPE_DIM
            o_gk = o_kr + GQA_HEADS * GQA_HEAD_DIM
            o_gv = o_gk + GQA_KV_HEADS * GQA_HEAD_DIM
            w_kr = w_in[:, o_ckv:o_kr]
            w_in_r = jnp.concatenate([w_in[:, :o_ckv], w_in[:, o_kr:], w_kr, _rotate_half_cols(w_kr)], axis=1).astype(BF16)
            c_ckv = MLA_Q_RANK // MLA_KV_RANK
            c_gq = o_ckv // LANES
            c_gk = c_gq + GQA_HEADS
            c_gv = c_gk + GQA_KV_HEADS
            c_kr = c_gv + GQA_KV_HEADS
            wq = mla_w_uq[a].reshape(MLA_Q_RANK, MLA_HEADS, MLA_QK_DIM)
            wq = jnp.concatenate([wq, _rotate_half_cols(wq[..., MLA_NOPE_DIM:])], axis=-1).transpose(1, 0, 2).astype(BF16)
            wkv = mla_w_ukv[a].reshape(MLA_KV_RANK, MLA_HEADS, MLA_NOPE_DIM + MLA_V_DIM).transpose(1, 0, 2).astype(BF16)

            y_l = _norm_mm(x2, norm1_g[i], mod[0], mod[1], lat_row, w_in_r, tn=w_in_r.shape[1])
            y_c = _norm_mm(ctx2, norm1_g[i], mod[0], mod[1], ctx_row_fn, w_in_r, tn=w_in_r.shape[1])

            mq = _mla_q(y_l, mla_q_norm_g[a], wq, cos_m, sin_m, nb, sl)
            mk, mv = _mla_kv(y_l, c_ckv, c_kr, mla_kv_norm_g[a], wkv, cos_m, sin_m, nb, sl)
            mkc, mvc = _mla_kv(y_c, c_ckv, c_kr, mla_kv_norm_g[a], wkv, one(tc, MLA_ROPE_DIM), zero(tc, MLA_ROPE_DIM), nb, tc)
            hd = GQA_HEAD_DIM
            gq = _head_prep(y_l, c_gq, GQA_HEADS, gqa_q_norm_g[a], cos_g, sin_g, nb, sl, True, True)
            gk = _head_prep(y_l, c_gk, GQA_KV_HEADS, gqa_k_norm_g[a], cos_g, sin_g, nb, sl, True, True)
            gv = _head_prep(y_l, c_gv, GQA_KV_HEADS, gqa_k_norm_g[a], cos_g, sin_g, nb, sl, False, False)
            gkc = _head_prep(y_c, c_gk, GQA_KV_HEADS, gqa_k_norm_g[a], one(tc, hd), zero(tc, hd), nb, tc, True, False)
            gvc = _head_prep(y_c, c_gv, GQA_KV_HEADS, gqa_k_norm_g[a], one(tc, hd), zero(tc, hd), nb, tc, False, False)

            o_m = _attention(mq, mkc, mk, mvc, mv, MLA_QK_DIM ** -0.5)
            o_g = _attention(gq, gkc, gk, gvc, gv, GQA_HEAD_DIM ** -0.5)
            w_out = attn_w_out[a].astype(BF16)
            n_m = MLA_HEADS * MLA_V_DIM
            x2 = _mm_residual([o_m, o_g], [w_out[:n_m], w_out[n_m:]], x2, mod[2], lat_row)
        else:
            mi = i // 2
            y = _norm_mm(x2, norm1_g[i], mod[0], mod[1], lat_row, conv_w_in[mi].astype(BF16), tn=1280)
            yc = _conv_mixer(y, cfm_dw_w[mi], cfm_dw_b[mi], cfm_ln_g[mi], cfm_ln_b[mi], sc_dw_w[mi], nb, sl)
            x2 = _mm_residual([yc], [conv_w_out[mi].astype(BF16)], x2, mod[2], lat_row)
        x2 = _peer(x2, mod, lat_row, gat_row, norm2_g[i], peer_w_q[i], peer_key1[i], peer_key2[i], peer_u[i], peer_v[i],
                   final_norm_g, final_norm=(i == depth - 1))
    return x2.reshape(nb, sl, d)
```

```python
import functools

import jax
import jax.numpy as jnp
from jax import lax
from jax.experimental import pallas as pl
from jax.experimental.pallas import tpu as pltpu

F32 = jnp.float32
BF16 = jnp.bfloat16

GRID_W = 64
ROPE_THETA = 10000.0
NORM_EPS = 1e-6
MLA_HEADS = 8
MLA_Q_RANK = 512
MLA_KV_RANK = 256
MLA_NOPE_DIM = 128
MLA_ROPE_DIM = 64
MLA_V_DIM = 128
MLA_QK_DIM = MLA_NOPE_DIM + MLA_ROPE_DIM
GQA_HEADS = 8
GQA_KV_HEADS = 2
GQA_HEAD_DIM = 128
CFM_WIDTH = 1024
CFM_KERNEL = 31
SC_WIDTH = 1024
SC_KERNEL = 3
PEER_HEADS = 8
PEER_KEY_DIM = 256
PEER_N_KEYS = 128
PEER_TOPK = 16
PEER_PICKS = PEER_HEADS * PEER_TOPK

LANES = 128
SUBLANES = 8
VMEM_LIMIT = 48 * 1024 * 1024

ROW_TILE = 256
ATTN_Q_TILE = 256
CONV_TILE = 256
CONV_HALO = 16
GATHER_TOKENS = 8


def _cparams(*sem):
    return pltpu.CompilerParams(dimension_semantics=sem, vmem_limit_bytes=VMEM_LIMIT)


def _mod_kernel(c_ref, w_ref, b_ref, o_ref):
    c = c_ref[...]
    a = (c * jax.nn.sigmoid(c)).astype(BF16)
    o_ref[...] = jnp.dot(a, w_ref[...].astype(BF16), preferred_element_type=F32) + b_ref[...]


def _modulation(cc, w, b):
    r, d = cc.shape
    n = w.shape[1]
    tn = 1024
    return pl.pallas_call(
        _mod_kernel,
        grid=(n // tn,),
        in_specs=[pl.BlockSpec((r, d), lambda j: (0, 0)),
                  pl.BlockSpec((d, tn), lambda j: (0, j)),
                  pl.BlockSpec((1, tn), lambda j: (0, j))],
        out_specs=pl.BlockSpec((r, tn), lambda j: (0, j)),
        out_shape=jax.ShapeDtypeStruct((r, n), F32),
        compiler_params=_cparams("arbitrary"),
        name="adaln_modulation",
    )(cc, w, b.reshape(1, n))


def _norm_mm_kernel(x_ref, g_ref, sh_ref, sc_ref, w_ref, o_ref, *rest, emit_h):
    h_scr = rest[-1]

    @pl.when(pl.program_id(1) == 0)
    def _():
        x = x_ref[...]
        y = x * lax.rsqrt(jnp.mean(x * x, axis=-1, keepdims=True) + NORM_EPS) * g_ref[...]
        h = y * (1 + sc_ref[...]) + sh_ref[...]
        h_scr[...] = h.astype(BF16)
        if emit_h:
            rest[0][...] = h

    o_ref[...] = jnp.dot(h_scr[...], w_ref[...], preferred_element_type=F32)


def _norm_mm(x, g, shift, scale, row_fn, w, tn, emit_h=False):
    m, d = x.shape
    n = w.shape[1]
    tm = ROW_TILE
    mod_spec = pl.BlockSpec((None, 1, d), lambda i, j: (row_fn(i), 0, 0))
    out_shape = [jax.ShapeDtypeStruct((m, n), F32)]
    out_specs = [pl.BlockSpec((tm, tn), lambda i, j: (i, j))]
    if emit_h:
        out_shape.append(jax.ShapeDtypeStruct((m, d), F32))
        out_specs.append(pl.BlockSpec((tm, d), lambda i, j: (i, 0)))
    res = pl.pallas_call(
        functools.partial(_norm_mm_kernel, emit_h=emit_h),
        grid=(m // tm, n // tn),
        in_specs=[pl.BlockSpec((tm, d), lambda i, j: (i, 0)),
                  pl.BlockSpec((1, d), lambda i, j: (0, 0)),
                  mod_spec, mod_spec,
                  pl.BlockSpec((d, tn), lambda i, j: (0, j))],
        out_specs=out_specs,
        out_shape=out_shape,
        scratch_shapes=[pltpu.VMEM((tm, d), BF16)],
        compiler_params=_cparams("arbitrary", "arbitrary"),
        name="norm_modulate_matmul",
    )(x, g.reshape(1, d), shift, scale, w)
    return res if emit_h else res[0]


def _mm_res_kernel(*refs, n_a):
    a_refs, w_refs = refs[:n_a], refs[n_a:2 * n_a]
    x_ref, gate_ref, o_ref = refs[2 * n_a:]
    acc = jnp.dot(a_refs[0][...], w_refs[0][...], preferred_element_type=F32)
    for a_ref, w_ref in zip(a_refs[1:], w_refs[1:]):
        acc = acc + jnp.dot(a_ref[...], w_ref[...], preferred_element_type=F32)
    o_ref[...] = x_ref[...] + gate_ref[...] * acc


def _mm_residual(a_list, w_list, x, gate, row_fn):
    m, d = x.shape
    tm = ROW_TILE
    n_a = len(a_list)
    in_specs = [pl.BlockSpec((tm, a.shape[1]), lambda i: (i, 0)) for a in a_list]
    in_specs += [pl.BlockSpec(w.shape, lambda i: (0, 0)) for w in w_list]
    in_specs += [pl.BlockSpec((tm, d), lambda i: (i, 0)),
                 pl.BlockSpec((None, 1, d), lambda i: (row_fn(i), 0, 0))]
    return pl.pallas_call(
        functools.partial(_mm_res_kernel, n_a=n_a),
        grid=(m // tm,),
        in_specs=in_specs,
        out_specs=pl.BlockSpec((tm, d), lambda i: (i, 0)),
        out_shape=jax.ShapeDtypeStruct((m, d), F32),
        compiler_params=_cparams("arbitrary"),
        name="matmul_gated_residual",
    )(*a_list, *w_list, x, gate)


def _rms(x, g):
    return x * lax.rsqrt(jnp.mean(x * x, axis=-1, keepdims=True) + NORM_EPS) * g


def _mla_q_kernel(cq_ref, g_ref, w_ref, cos_ref, sin_ref, o_ref, xn_scr):
    @pl.when(pl.program_id(2) == 0)
    def _():
        xn_scr[...] = _rms(cq_ref[...], g_ref[...]).astype(BF16)

    y = jnp.dot(xn_scr[...], w_ref[...], preferred_element_type=F32)
    nope = y[:, :MLA_NOPE_DIM]
    r = y[:, MLA_NOPE_DIM:MLA_QK_DIM]
    rr = y[:, MLA_QK_DIM:]
    o_ref[...] = jnp.concatenate([nope, r * cos_ref[...] + rr * sin_ref[...]], axis=-1).astype(BF16)


def _mla_q(y_in, g, w, cos, sin, nb, sl):
    tm = ROW_TILE
    return pl.pallas_call(
        _mla_q_kernel,
        grid=(nb, sl // tm, MLA_HEADS),
        in_specs=[pl.BlockSpec((tm, MLA_Q_RANK), lambda b, i, h: (b * (sl // tm) + i, 0)),
                  pl.BlockSpec((1, MLA_Q_RANK), lambda b, i, h: (0, 0)),
                  pl.BlockSpec((None, MLA_Q_RANK, 2 * LANES), lambda b, i, h: (h, 0, 0)),
                  pl.BlockSpec((tm, MLA_ROPE_DIM), lambda b, i, h: (i, 0)),
                  pl.BlockSpec((tm, MLA_ROPE_DIM), lambda b, i, h: (i, 0))],
        out_specs=pl.BlockSpec((None, None, tm, MLA_QK_DIM), lambda b, i, h: (b, h, i, 0)),
        out_shape=jax.ShapeDtypeStruct((nb, MLA_HEADS, sl, MLA_QK_DIM), BF16),
        scratch_shapes=[pltpu.VMEM((tm, MLA_Q_RANK), BF16)],
        compiler_params=_cparams("arbitrary", "arbitrary", "arbitrary"),
        name="mla_query_prep",
    )(y_in, g.reshape(1, -1), w, cos, sin)


def _mla_kv_kernel(ckv_ref, kr_ref, g_ref, w_ref, cos_ref, sin_ref, k_ref, v_ref, xn_scr):
    @pl.when(pl.program_id(2) == 0)
    def _():
        xn_scr[...] = _rms(ckv_ref[...], g_ref[...]).astype(BF16)

    y = jnp.dot(xn_scr[...], w_ref[...], preferred_element_type=F32)
    krb = kr_ref[...]
    kr = krb[:, :MLA_ROPE_DIM] * cos_ref[...] + krb[:, MLA_ROPE_DIM:] * sin_ref[...]
    k_ref[...] = jnp.concatenate([y[:, :MLA_NOPE_DIM], kr], axis=-1).astype(BF16)
    v_ref[...] = y[:, MLA_NOPE_DIM:].astype(BF16)


def _mla_kv(y_in, ckv_col, kr_col, g, w, cos, sin, nb, sl):
    tm = ROW_TILE
    return pl.pallas_call(
        _mla_kv_kernel,
        grid=(nb, sl // tm, MLA_HEADS),
        in_specs=[pl.BlockSpec((tm, MLA_KV_RANK), lambda b, i, h: (b * (sl // tm) + i, ckv_col)),
                  pl.BlockSpec((tm, LANES), lambda b, i, h: (b * (sl // tm) + i, kr_col)),
                  pl.BlockSpec((1, MLA_KV_RANK), lambda b, i, h: (0, 0)),
                  pl.BlockSpec((None, MLA_KV_RANK, 2 * LANES), lambda b, i, h: (h, 0, 0)),
                  pl.BlockSpec((tm, MLA_ROPE_DIM), lambda b, i, h: (i, 0)),
                  pl.BlockSpec((tm, MLA_ROPE_DIM), lambda b, i, h: (i, 0))],
        out_specs=[pl.BlockSpec((None, None, tm, MLA_QK_DIM), lambda b, i, h: (b, h, i, 0)),
                   pl.BlockSpec((None, None, tm, MLA_V_DIM), lambda b, i, h: (b, h, i, 0))],
        out_shape=[jax.ShapeDtypeStruct((nb, MLA_HEADS, sl, MLA_QK_DIM), BF16),
                   jax.ShapeDtypeStruct((nb, MLA_HEADS, sl, MLA_V_DIM), BF16)],
        scratch_shapes=[pltpu.VMEM((tm, MLA_KV_RANK), BF16)],
        compiler_params=_cparams("arbitrary", "arbitrary", "arbitrary"),
        name="mla_key_value_prep",
    )(y_in, y_in, g.reshape(1, -1), w, cos, sin)


def _head_kernel(x_ref, g_ref, cos_ref, sin_ref, o_ref, *, norm, rope):
    x = x_ref[...]
    if norm:
        x = _rms(x, g_ref[...])
    if rope:
        q = GQA_HEAD_DIM // 4
        lane = lax.broadcasted_iota(jnp.int32, x.shape, 1)
        first = (lane % (2 * q)) < q
        rot = jnp.where(first, -pltpu.roll(x, GQA_HEAD_DIM - q, 1), pltpu.roll(x, q, 1))
        x = x * cos_ref[...] + rot * sin_ref[...]
    o_ref[...] = x.astype(BF16)


def _head_prep(y_in, col0, n_heads, g, cos, sin, nb, sl, norm, rope):
    tm = ROW_TILE
    return pl.pallas_call(
        functools.partial(_head_kernel, norm=norm, rope=rope),
        grid=(nb, sl // tm, n_heads),
        in_specs=[pl.BlockSpec((tm, GQA_HEAD_DIM), lambda b, i, h: (b * (sl // tm) + i, col0 + h)),
                  pl.BlockSpec((1, GQA_HEAD_DIM), lambda b, i, h: (0, 0)),
                  pl.BlockSpec((tm, GQA_HEAD_DIM), lambda b, i, h: (i, 0)),
                  pl.BlockSpec((tm, GQA_HEAD_DIM), lambda b, i, h: (i, 0))],
        out_specs=pl.BlockSpec((None, None, tm, GQA_HEAD_DIM), lambda b, i, h: (b, h, i, 0)),
        out_shape=jax.ShapeDtypeStruct((nb, n_heads, sl, GQA_HEAD_DIM), BF16),
        compiler_params=_cparams("arbitrary", "arbitrary", "arbitrary"),
        name="gqa_head_prep",
    )(y_in, g.reshape(1, -1), cos, sin)


def _attn_kernel(q_ref, kc_ref, kl_ref, vc_ref, vl_ref, o_ref, *, scale):
    q = q_ref[...]
    dn = (((1,), (1,)), ((), ()))
    sc = lax.dot_general(q, kc_ref[...], dn, preferred_element_type=F32) * scale
    sl = lax.dot_general(q, kl_ref[...], dn, preferred_element_type=F32) * scale
    m = jnp.maximum(jnp.max(sc, axis=-1, keepdims=True), jnp.max(sl, axis=-1, keepdims=True))
    pc = jnp.exp(sc - m)
    pl_ = jnp.exp(sl - m)
    denom = jnp.sum(pc, axis=-1, keepdims=True) + jnp.sum(pl_, axis=-1, keepdims=True)
    o = jnp.dot(pc.astype(BF16), vc_ref[...], preferred_element_type=F32)
    o = o + jnp.dot(pl_.astype(BF16), vl_ref[...], preferred_element_type=F32)
    o_ref[...] = (o / denom).astype(BF16)


def _attention(q, kc, kl, vc, vl, scale):
    nb, nh, sl, dk = q.shape
    hk, tc, tl, dv = kc.shape[1], kc.shape[2], kl.shape[2], vc.shape[3]
    grp = nh // hk
    tq = ATTN_Q_TILE
    kv_spec = lambda t, d: pl.BlockSpec((None, None, t, d), lambda b, h, i: (b, h // grp, 0, 0))
    return pl.pallas_call(
        functools.partial(_attn_kernel, scale=scale),
        grid=(nb, nh, sl // tq),
        in_specs=[pl.BlockSpec((None, None, tq, dk), lambda b, h, i: (b, h, i, 0)),
                  kv_spec(tc, dk), kv_spec(tl, dk), kv_spec(tc, dv), kv_spec(tl, dv)],
        out_specs=pl.BlockSpec((tq, dv), lambda b, h, i: (b * (sl // tq) + i, h)),
        out_shape=jax.ShapeDtypeStruct((nb * sl, nh * dv), BF16),
        compiler_params=_cparams("arbitrary", "arbitrary", "arbitrary"),
        name="softmax_attention",
    )(q, kc, kl, vc, vl)


def _conv_kernel(a_ref, ag_ref, b_ref, c_ref, u_ref,
                 ap_ref, agp_ref, cp_ref, up_ref, an_ref, agn_ref, cn_ref, un_ref,
                 cw_ref, cb_ref, lg_ref, lb_ref, sw_ref, o_ref, ext_scr, ext2_scr):
    i = pl.program_id(1)
    n = pl.num_programs(1)
    ts = a_ref.shape[0]
    hl = CONV_HALO
    has_prev = (i > 0).astype(F32)
    has_next = (i + 1 < n).astype(F32)
    glu = lambda a, g: a * jax.nn.sigmoid(g)
    ext_scr[pl.ds(0, hl), :] = glu(ap_ref[...], agp_ref[...]) * has_prev
    ext_scr[pl.ds(hl, ts), :] = glu(a_ref[...], ag_ref[...])
    ext_scr[pl.ds(hl + ts, hl), :] = glu(an_ref[...], agn_ref[...]) * has_next
    ext2_scr[pl.ds(0, hl), :] = cp_ref[...] * up_ref[...] * has_prev
    ext2_scr[pl.ds(hl, ts), :] = c_ref[...] * u_ref[...]
    ext2_scr[pl.ds(hl + ts, hl), :] = cn_ref[...] * un_ref[...] * has_next

    pad = (CFM_KERNEL - 1) // 2
    acc = cw_ref[pl.ds(0, 1), :] * ext_scr[pl.ds(hl - pad, ts), :]
    for k in range(1, CFM_KERNEL):
        acc = acc + cw_ref[pl.ds(k, 1), :] * ext_scr[pl.ds(hl - pad + k, ts), :]
    acc = acc + cb_ref[...]
    mu = jnp.mean(acc, axis=-1, keepdims=True)
    xc = acc - mu
    yn = xc * lax.rsqrt(jnp.mean(xc * xc, axis=-1, keepdims=True) + NORM_EPS) * lg_ref[...] + lb_ref[...]
    yc = yn * jax.nn.sigmoid(yn)

    pad2 = (SC_KERNEL - 1) // 2
    acc2 = sw_ref[pl.ds(0, 1), :] * ext2_scr[pl.ds(hl - pad2, ts), :]
    for k in range(1, SC_KERNEL):
        acc2 = acc2 + sw_ref[pl.ds(k, 1), :] * ext2_scr[pl.ds(hl - pad2 + k, ts), :]
    yd = b_ref[...] * acc2
    o_ref[...] = jnp.concatenate([yc, yd], axis=-1).astype(BF16)


def _conv_mixer(y_in, cfm_w, cfm_b, ln_g, ln_b, sc_w, nb, sl):
    ts, hl, cw = CONV_TILE, CONV_HALO, CFM_WIDTH
    nt = sl // ts
    rb = ts // hl
    last = nb * sl // hl - 1
    cur = lambda col: pl.BlockSpec((ts, cw), lambda b, i: (b * nt + i, col))
    prev = lambda col: pl.BlockSpec((hl, cw), lambda b, i: (jnp.maximum((b * nt + i) * rb - 1, 0), col))
    nxt = lambda col: pl.BlockSpec((hl, cw), lambda b, i: (jnp.minimum((b * nt + i + 1) * rb, last), col))
    vec = lambda r: pl.BlockSpec((r, cw), lambda b, i: (0, 0))
    return pl.pallas_call(
        _conv_kernel,
        grid=(nb, nt),
        in_specs=[cur(0), cur(1), cur(2), cur(3), cur(4),
                  prev(0), prev(1), prev(3), prev(4), nxt(0), nxt(1), nxt(3), nxt(4),
                  vec(CFM_KERNEL), vec(1), vec(1), vec(1), vec(SC_KERNEL)],
        out_specs=pl.BlockSpec((ts, 2 * cw), lambda b, i: (b * nt + i, 0)),
        out_shape=jax.ShapeDtypeStruct((nb * sl, 2 * cw), BF16),
        scratch_shapes=[pltpu.VMEM((ts + 2 * hl, cw), F32), pltpu.VMEM((ts + 2 * hl, cw), F32)],
        compiler_params=_cparams("arbitrary", "arbitrary"),
        name="conv_mixer",
    )(*([y_in] * 13), cfm_w, cfm_b.reshape(1, cw), ln_g.reshape(1, cw), ln_b.reshape(1, cw), sc_w)


_NO_RANK = 1 << 20


def _top_k_rows(s, rank=None, payload=None):
    if rank is None:
        rank = lax.broadcasted_iota(jnp.int32, s.shape, 0)
    vals, idxs = [], []
    for _ in range(PEER_TOPK):
        m = jnp.max(s, axis=0, keepdims=True)
        idx = jnp.min(jnp.where(s == m, rank, _NO_RANK), axis=0, keepdims=True)
        hit = rank == idx
        vals.append(m)
        idxs.append(idx if payload is None else jnp.sum(jnp.where(hit, payload, 0), axis=0, keepdims=True))
        s = jnp.where(hit, -jnp.inf, s)
    return jnp.concatenate(vals, axis=0), jnp.concatenate(idxs, axis=0)


def _pair_candidates(v1, i1, v2, i2):
    k = PEER_TOPK
    r = lax.broadcasted_iota(jnp.int32, (SUBLANES, v1.shape[1]), 0)
    lo, hi = slice(0, SUBLANES), slice(SUBLANES, 2 * SUBLANES)
    groups = []

    def row_a(a, b_rows, n_valid):
        b0 = b_rows.start
        groups.append((v1[a:a + 1] + v2[b_rows], i1[a:a + 1] * PEER_N_KEYS + i2[b_rows], a * k + b0 + r,
                       None if n_valid == SUBLANES else r < n_valid))

    def col_b(b, a_rows, valid):
        a0 = a_rows.start
        groups.append((v1[a_rows] + v2[b:b + 1], i1[a_rows] * PEER_N_KEYS + i2[b:b + 1], (a0 + r) * k + b, valid))

    row_a(0, lo, 8)
    row_a(0, hi, 8)
    row_a(1, lo, 8)
    col_b(0, hi, None)
    row_a(2, lo, k // 3)
    row_a(3, lo, k // 4)
    row_a(4, lo, k // 5)
    col_b(0, lo, r >= 5)
    col_b(1, lo, r >= 5)
    vals = jnp.concatenate([v if ok is None else jnp.where(ok, v, -jnp.inf) for v, _, _, ok in groups], axis=0)
    flat = jnp.concatenate([f if ok is None else jnp.where(ok, f, _NO_RANK) for _, _, f, ok in groups], axis=0)
    ids = jnp.concatenate([c if ok is None else jnp.where(ok, c, 0) for _, c, _, ok in groups], axis=0)
    return vals, flat, ids


def _route_kernel(q_ref, k1_ref, k2_ref, ids_ref, gate_ref):
    half = PEER_KEY_DIM // 2
    q = q_ref[...].astype(BF16)
    dn = (((1,), (1,)), ((), ()))
    s1 = lax.dot_general(k1_ref[...], q[:, :half], dn, preferred_element_type=F32)
    s2 = lax.dot_general(k2_ref[...], q[:, half:], dn, preferred_element_type=F32)
    v1, i1 = _top_k_rows(s1)
    v2, i2 = _top_k_rows(s2)
    cand, flat, cand_id = _pair_candidates(v1, i1, v2, i2)
    best, ids = _top_k_rows(cand, flat, cand_id)
    e = jnp.exp(best - jnp.max(best, axis=0, keepdims=True))
    gate_ref[...] = e / jnp.sum(e, axis=0, keepdims=True)
    ids_ref[...] = ids


def _peer_route(q, key1, key2):
    m = q.shape[0]
    tm = ROW_TILE
    out_spec = pl.BlockSpec((PEER_TOPK, tm), lambda i, h: (h, i))
    return pl.pallas_call(
        _route_kernel,
        grid=(m // tm, PEER_HEADS),
        in_specs=[pl.BlockSpec((tm, PEER_KEY_DIM), lambda i, h: (i, h)),
                  pl.BlockSpec(key1.shape, lambda i, h: (0, 0)),
                  pl.BlockSpec(key2.shape, lambda i, h: (0, 0))],
        out_specs=[out_spec, out_spec],
        out_shape=[jax.ShapeDtypeStruct((PEER_PICKS, m), jnp.int32),
                   jax.ShapeDtypeStruct((PEER_PICKS, m), F32)],
        compiler_params=_cparams("arbitrary", "arbitrary"),
        name="peer_route",
    )(q, key1, key2)


def _gather_rows(tt):
    return tt * PEER_PICKS


def _issue_rows(ids_ref, tab_ref, buf_ref, sem, slot, rows, row0, n_rows, active):
    groups = rows // SUBLANES
    unroll = 2 * SUBLANES

    def body(g, carry):
        base = pl.multiple_of(row0 + g * unroll, unroll)
        blk = buf_ref.at[pl.ds(slot * groups + row0 // SUBLANES + g * 2, 2)]
        for k in range(unroll):
            pltpu.make_async_copy(tab_ref.at[ids_ref[base + k]], blk.at[k // SUBLANES, pl.ds(k % SUBLANES, 1)],
                                  sem.at[slot]).start(priority=k % 2)
        return carry

    lax.fori_loop(0, active * (n_rows // unroll), body, 0)


def _wait_rows(buf_ref, sem, slot, rows):
    groups = rows // SUBLANES
    half = buf_ref.at[pl.ds(slot * groups, groups)]
    pltpu.make_async_copy(half, half, sem.at[slot]).wait()


def _gather_kernel(ids_cur, ids_nxt, h_ref, gate_ref, x_ref, mg_ref, fg_ref, uv_ref, o_ref, buf, sem, *, final_norm):
    i = pl.program_id(0)
    n = pl.num_programs(0)
    tt = h_ref.shape[0]
    rows = _gather_rows(tt)
    slot = i % 2

    @pl.when(i == 0)
    def _():
        _issue_rows(ids_cur, uv_ref, buf, sem, 0, rows, 0, rows, 1)

    _wait_rows(buf, sem, slot, rows)
    has_next = (i + 1 < n).astype(jnp.int32)
    pg = PEER_PICKS // SUBLANES
    d = h_ref.shape[1]
    for t in range(tt):
        _issue_rows(ids_nxt, uv_ref, buf, sem, 1 - slot, rows, t * PEER_PICKS, PEER_PICKS, has_next)
        w = buf[pl.ds(slot * (rows // SUBLANES) + t * pg, pg)].reshape(PEER_PICKS, d)
        ut = lax.bitcast_convert_type(w & jnp.uint32(0xFFFF0000), F32)
        vt = lax.bitcast_convert_type(w << 16, F32)
        s = jnp.sum(ut * h_ref[pl.ds(t, 1), :], axis=-1, keepdims=True)
        a = jax.nn.gelu(s) * gate_ref[:, pl.ds(t, 1)]
        y = jnp.sum(vt * a, axis=0, keepdims=True)
        o_ref[pl.ds(t, 1), :] = x_ref[pl.ds(t, 1), :] + mg_ref[...] * y
    if final_norm:
        o_ref[...] = _rms(o_ref[...], fg_ref[...])


def _pack_expert_tables(u, v):
    hi = lax.bitcast_convert_type(u.astype(BF16), jnp.uint16).astype(jnp.uint32)
    lo = lax.bitcast_convert_type(v.astype(BF16), jnp.uint16).astype(jnp.uint32)
    return ((hi << 16) | lo).reshape(u.shape[0], 1, u.shape[1])


def _peer_mix(h2, ids_t, gate_t, x, mod_gate, row_fn, final_g, uv, final_norm):
    m, d = h2.shape
    tt = GATHER_TOKENS
    steps = m // tt
    rows = _gather_rows(tt)
    ids = ids_t.T.reshape(m * PEER_PICKS)
    gate = gate_t.reshape(PEER_PICKS, steps, tt).transpose(1, 0, 2)
    ids_spec = lambda f: pl.BlockSpec((rows,), f, memory_space=pltpu.SMEM)
    tok_spec = pl.BlockSpec((tt, d), lambda i: (i, 0))
    return pl.pallas_call(
        functools.partial(_gather_kernel, final_norm=final_norm),
        grid=(steps,),
        in_specs=[ids_spec(lambda i: (i,)),
                  ids_spec(lambda i: (jnp.minimum(i + 1, steps - 1),)),
                  tok_spec,
                  pl.BlockSpec((None, PEER_PICKS, tt), lambda i: (i, 0, 0)),
                  tok_spec,
                  pl.BlockSpec((None, 1, d), lambda i: (row_fn(i), 0, 0)),
                  pl.BlockSpec((1, d), lambda i: (0, 0)),
                  pl.BlockSpec(memory_space=pl.ANY)],
        out_specs=tok_spec,
        out_shape=jax.ShapeDtypeStruct((m, d), F32),
        scratch_shapes=[pltpu.VMEM((2 * rows // SUBLANES, SUBLANES, d), jnp.uint32), pltpu.SemaphoreType.DMA((2,))],
        compiler_params=_cparams("arbitrary"),
        name="peer_gather_mix",
    )(ids, ids, h2, gate, x, mod_gate, final_g.reshape(1, d), uv)


def _rope_tables(n_rows, d_rot):
    row = jnp.broadcast_to(jnp.arange(n_rows, dtype=F32)[:, None], (n_rows, GRID_W)).reshape(-1)
    col = jnp.broadcast_to(jnp.arange(GRID_W, dtype=F32)[None, :], (n_rows, GRID_W)).reshape(-1)
    quarter = d_rot // 4
    inv_freq = ROPE_THETA ** (-jnp.arange(quarter, dtype=F32) / quarter)
    ar = row[:, None] * inv_freq
    ac = col[:, None] * inv_freq
    ang = jnp.concatenate([ar, ar, ac, ac], axis=-1)
    return jnp.cos(ang), jnp.sin(ang)


def _rotate_half_cols(w):
    r1, r2, c1, c2 = jnp.split(w, 4, axis=-1)
    return jnp.concatenate([-r2, r1, -c2, c1], axis=-1)


def _peer(x2, mod, tile_row_fn, gather_row_fn, norm_g, w_q, key1, key2, u, v, final_g, final_norm):
    d = x2.shape[1]
    q, h2 = _norm_mm(x2, norm_g, mod[3], mod[4], tile_row_fn, w_q.astype(BF16), tn=d, emit_h=True)
    ids_t, gate_t = _peer_route(q, key1.astype(BF16), key2.astype(BF16))
    return _peer_mix(h2, ids_t, gate_t, x2, mod[5], gather_row_fn, final_g, _pack_expert_tables(u, v), final_norm)


def kernel(x, c, ctx, c_ctx, ada_w, ada_b, norm1_g, norm2_g, attn_w_in, mla_q_norm_g, mla_w_uq, mla_kv_norm_g, mla_w_ukv, gqa_q_norm_g, gqa_k_norm_g, attn_w_out, conv_w_in, cfm_dw_w, cfm_dw_b, cfm_ln_g, cfm_ln_b, sc_dw_w, conv_w_out, peer_w_q, peer_key1, peer_key2, peer_u, peer_v, final_norm_g):
    nb, sl, d = x.shape
    tc = ctx.shape[1]
    depth = ada_w.shape[0]
    assert sl % ROW_TILE == 0 and tc % ROW_TILE == 0 and sl % GRID_W == 0 and d == 2048

    x2 = x.reshape(nb * sl, d)
    ctx2 = ctx.reshape(nb * tc, d)
    ctx_row = nb
    cc = jnp.zeros((SUBLANES, d), F32).at[:nb].set(c).at[ctx_row].set(c_ctx)
    lat_row = lambda i: (i * ROW_TILE) // sl
    ctx_row_fn = lambda i: ctx_row
    gat_row = lambda i: (i * GATHER_TOKENS) // sl

    cos_m, sin_m = _rope_tables(sl // GRID_W, MLA_ROPE_DIM)
    cos_g, sin_g = _rope_tables(sl // GRID_W, GQA_HEAD_DIM)
    one = lambda n, w: jnp.ones((n, w), F32)
    zero = lambda n, w: jnp.zeros((n, w), F32)

    for i in range(depth):
        mod = _modulation(cc, ada_w[i], ada_b[i]).reshape(SUBLANES, 6, 1, d).transpose(1, 0, 2, 3)
        if i % 2 == 0:
            a = i // 2
            w_in = attn_w_in[a]
            o_cq, o_ckv = MLA_Q_RANK, MLA_Q_RANK + MLA_KV_RANK
            o_kr = o_ckv + MLA_ROPE_DIM
            o_gk = o_kr + GQA_HEADS * GQA_HEAD_DIM
            o_gv = o_gk + GQA_KV_HEADS * GQA_HEAD_DIM
            w_kr = w_in[:, o_ckv:o_kr]
            w_in_r = jnp.concatenate([w_in[:, :o_ckv], w_in[:, o_kr:], w_kr, _rotate_half_cols(w_kr)], axis=1).astype(BF16)
            c_ckv = MLA_Q_RANK // MLA_KV_RANK
            c_gq = o_ckv // LANES
            c_gk = c_gq + GQA_HEADS
            c_gv = c_gk + GQA_KV_HEADS
            c_kr = c_gv + GQA_KV_HEADS
            wq = mla_w_uq[a].reshape(MLA_Q_RANK, MLA_HEADS, MLA_QK_DIM)
            wq = jnp.concatenate([wq, _rotate_half_cols(wq[..., MLA_NOPE_DIM:])], axis=-1).transpose(1, 0, 2).astype(BF16)
            wkv = mla_w_ukv[a].reshape(MLA_KV_RANK, MLA_HEADS, MLA_NOPE_DIM + MLA_V_DIM).transpose(1, 0, 2).astype(BF16)

            y_l = _norm_mm(x2, norm1_g[i], mod[0], mod[1], lat_row, w_in_r, tn=w_in_r.shape[1])
            y_c = _norm_mm(ctx2, norm1_g[i], mod[0], mod[1], ctx_row_fn, w_in_r, tn=w_in_r.shape[1])

            mq = _mla_q(y_l, mla_q_norm_g[a], wq, cos_m, sin_m, nb, sl)
            mk, mv = _mla_kv(y_l, c_ckv, c_kr, mla_kv_norm_g[a], wkv, cos_m, sin_m, nb, sl)
            mkc, mvc = _mla_kv(y_c, c_ckv, c_kr, mla_kv_norm_g[a], wkv, one(tc, MLA_ROPE_DIM), zero(tc, MLA_ROPE_DIM), nb, tc)
            hd = GQA_HEAD_DIM
            gq = _head_prep(y_l, c_gq, GQA_HEADS, gqa_q_norm_g[a], cos_g, sin_g, nb, sl, True, True)
            gk = _head_prep(y_l, c_gk, GQA_KV_HEADS, gqa_k_norm_g[a], cos_g, sin_g, nb, sl, True, True)
            gv = _head_prep(y_l, c_gv, GQA_KV_HEADS, gqa_k_norm_g[a], cos_g, sin_g, nb, sl, False, False)
            gkc = _head_prep(y_c, c_gk, GQA_KV_HEADS, gqa_k_norm_g[a], one(tc, hd), zero(tc, hd), nb, tc, True, False)
            gvc = _head_prep(y_c, c_gv, GQA_KV_HEADS, gqa_k_norm_g[a], one(tc, hd), zero(tc, hd), nb, tc, False, False)

            o_m = _attention(mq, mkc, mk, mvc, mv, MLA_QK_DIM ** -0.5)
            o_g = _attention(gq, gkc, gk, gvc, gv, GQA_HEAD_DIM ** -0.5)
            w_out = attn_w_out[a].astype(BF16)
            n_m = MLA_HEADS * MLA_V_DIM
            x2 = _mm_residual([o_m, o_g], [w_out[:n_m], w_out[n_m:]], x2, mod[2], lat_row)
        else:
            mi = i // 2
            y = _norm_mm(x2, norm1_g[i], mod[0], mod[1], lat_row, conv_w_in[mi].astype(BF16), tn=1280)
            yc = _conv_mixer(y, cfm_dw_w[mi], cfm_dw_b[mi], cfm_ln_g[mi], cfm_ln_b[mi], sc_dw_w[mi], nb, sl)
            x2 = _mm_residual([yc], [conv_w_out[mi].astype(BF16)], x2, mod[2], lat_row)
        x2 = _peer(x2, mod, lat_row, gat_row, norm2_g[i], peer_w_q[i], peer_key1[i], peer_key2[i], peer_u[i], peer_v[i],
                   final_norm_g, final_norm=(i == depth - 1))
    return x2.reshape(nb, sl, d)
```

```python
import functools

import jax
import jax.numpy as jnp
from jax import lax
from jax.experimental import pallas as pl
from jax.experimental.pallas import tpu as pltpu

F32 = jnp.float32
BF16 = jnp.bfloat16

GRID_W = 64
ROPE_THETA = 10000.0
NORM_EPS = 1e-6
MLA_HEADS = 8
MLA_Q_RANK = 512
MLA_KV_RANK = 256
MLA_NOPE_DIM = 128
MLA_ROPE_DIM = 64
MLA_V_DIM = 128
MLA_QK_DIM = MLA_NOPE_DIM + MLA_ROPE_DIM
GQA_HEADS = 8
GQA_KV_HEADS = 2
GQA_HEAD_DIM = 128
CFM_WIDTH = 1024
CFM_KERNEL = 31
SC_WIDTH = 1024
SC_KERNEL = 3
PEER_HEADS = 8
PEER_KEY_DIM = 256
PEER_N_KEYS = 128
PEER_TOPK = 16
PEER_PICKS = PEER_HEADS * PEER_TOPK

LANES = 128
SUBLANES = 8
VMEM_LIMIT = 48 * 1024 * 1024

ROW_TILE = 256
ATTN_Q_TILE = 256
CONV_TILE = 256
CONV_HALO = 16
GATHER_TOKENS = 8


def _cparams(*sem):
    return pltpu.CompilerParams(dimension_semantics=sem, vmem_limit_bytes=VMEM_LIMIT)


def _mod_kernel(c_ref, w_ref, b_ref, o_ref):
    c = c_ref[...]
    a = (c * jax.nn.sigmoid(c)).astype(BF16)
    o_ref[...] = jnp.dot(a, w_ref[...].astype(BF16), preferred_element_type=F32) + b_ref[...]


def _modulation(cc, w, b):
    r, d = cc.shape
    n = w.shape[1]
    tn = 1024
    return pl.pallas_call(
        _mod_kernel,
        grid=(n // tn,),
        in_specs=[pl.BlockSpec((r, d), lambda j: (0, 0)),
                  pl.BlockSpec((d, tn), lambda j: (0, j)),
                  pl.BlockSpec((1, tn), lambda j: (0, j))],
        out_specs=pl.BlockSpec((r, tn), lambda j: (0, j)),
        out_shape=jax.ShapeDtypeStruct((r, n), F32),
        compiler_params=_cparams("arbitrary"),
        name="adaln_modulation",
    )(cc, w, b.reshape(1, n))


def _norm_mm_kernel(x_ref, g_ref, sh_ref, sc_ref, w_ref, o_ref, *rest, emit_h):
    h_scr = rest[-1]

    @pl.when(pl.program_id(1) == 0)
    def _():
        x = x_ref[...]
        y = x * lax.rsqrt(jnp.mean(x * x, axis=-1, keepdims=True) + NORM_EPS) * g_ref[...]
        h = y * (1 + sc_ref[...]) + sh_ref[...]
        h_scr[...] = h.astype(BF16)
        if emit_h:
            rest[0][...] = h

    o_ref[...] = jnp.dot(h_scr[...], w_ref[...], preferred_element_type=F32)


def _norm_mm(x, g, shift, scale, row_fn, w, tn, emit_h=False):
    m, d = x.shape
    n = w.shape[1]
    tm = ROW_TILE
    mod_spec = pl.BlockSpec((None, 1, d), lambda i, j: (row_fn(i), 0, 0))
    out_shape = [jax.ShapeDtypeStruct((m, n), F32)]
    out_specs = [pl.BlockSpec((tm, tn), lambda i, j: (i, j))]
    if emit_h:
        out_shape.append(jax.ShapeDtypeStruct((m, d), F32))
        out_specs.append(pl.BlockSpec((tm, d), lambda i, j: (i, 0)))
    res = pl.pallas_call(
        functools.partial(_norm_mm_kernel, emit_h=emit_h),
        grid=(m // tm, n // tn),
        in_specs=[pl.BlockSpec((tm, d), lambda i, j: (i, 0)),
                  pl.BlockSpec((1, d), lambda i, j: (0, 0)),
                  mod_spec, mod_spec,
                  pl.BlockSpec((d, tn), lambda i, j: (0, j))],
        out_specs=out_specs,
        out_shape=out_shape,
        scratch_shapes=[pltpu.VMEM((tm, d), BF16)],
        compiler_params=_cparams("arbitrary", "arbitrary"),
        name="norm_modulate_matmul",
    )(x, g.reshape(1, d), shift, scale, w)
    return res if emit_h else res[0]


def _mm_res_kernel(*refs, n_a):
    a_refs, w_refs = refs[:n_a], refs[n_a:2 * n_a]
    x_ref, gate_ref, o_ref = refs[2 * n_a:]
    acc = jnp.dot(a_refs[0][...], w_refs[0][...], preferred_element_type=F32)
    for a_ref, w_ref in zip(a_refs[1:], w_refs[1:]):
        acc = acc + jnp.dot(a_ref[...], w_ref[...], preferred_element_type=F32)
    o_ref[...] = x_ref[...] + gate_ref[...] * acc


def _mm_residual(a_list, w_list, x, gate, row_fn):
    m, d = x.shape
    tm = ROW_TILE
    n_a = len(a_list)
    in_specs = [pl.BlockSpec((tm, a.shape[1]), lambda i: (i, 0)) for a in a_list]
    in_specs += [pl.BlockSpec(w.shape, lambda i: (0, 0)) for w in w_list]
    in_specs += [pl.BlockSpec((tm, d), lambda i: (i, 0)),
                 pl.BlockSpec((None, 1, d), lambda i: (row_fn(i), 0, 0))]
    return pl.pallas_call(
        functools.partial(_mm_res_kernel, n_a=n_a),
        grid=(m // tm,),
        in_specs=in_specs,
        out_specs=pl.BlockSpec((tm, d), lambda i: (i, 0)),
        out_shape=jax.ShapeDtypeStruct((m, d), F32),
        compiler_params=_cparams("arbitrary"),
        name="matmul_gated_residual",
    )(*a_list, *w_list, x, gate)


def _rms(x, g):
    return x * lax.rsqrt(jnp.mean(x * x, axis=-1, keepdims=True) + NORM_EPS) * g


def _mla_q_kernel(cq_ref, g_ref, w_ref, cos_ref, sin_ref, o_ref, xn_scr):
    @pl.when(pl.program_id(2) == 0)
    def _():
        xn_scr[...] = _rms(cq_ref[...], g_ref[...]).astype(BF16)

    y = jnp.dot(xn_scr[...], w_ref[...], preferred_element_type=F32)
    nope = y[:, :MLA_NOPE_DIM]
    r = y[:, MLA_NOPE_DIM:MLA_QK_DIM]
    rr = y[:, MLA_QK_DIM:]
    o_ref[...] = jnp.concatenate([nope, r * cos_ref[...] + rr * sin_ref[...]], axis=-1).astype(BF16)


def _mla_q(y_in, g, w, cos, sin, nb, sl):
    tm = ROW_TILE
    return pl.pallas_call(
        _mla_q_kernel,
        grid=(nb, sl // tm, MLA_HEADS),
        in_specs=[pl.BlockSpec((tm, MLA_Q_RANK), lambda b, i, h: (b * (sl // tm) + i, 0)),
                  pl.BlockSpec((1, MLA_Q_RANK), lambda b, i, h: (0, 0)),
                  pl.BlockSpec((None, MLA_Q_RANK, 2 * LANES), lambda b, i, h: (h, 0, 0)),
                  pl.BlockSpec((tm, MLA_ROPE_DIM), lambda b, i, h: (i, 0)),
                  pl.BlockSpec((tm, MLA_ROPE_DIM), lambda b, i, h: (i, 0))],
        out_specs=pl.BlockSpec((None, None, tm, MLA_QK_DIM), lambda b, i, h: (b, h, i, 0)),
        out_shape=jax.ShapeDtypeStruct((nb, MLA_HEADS, sl, MLA_QK_DIM), BF16),
        scratch_shapes=[pltpu.VMEM((tm, MLA_Q_RANK), BF16)],
        compiler_params=_cparams("arbitrary", "arbitrary", "arbitrary"),
        name="mla_query_prep",
    )(y_in, g.reshape(1, -1), w, cos, sin)


def _mla_kv_kernel(ckv_ref, kr_ref, g_ref, w_ref, cos_ref, sin_ref, k_ref, v_ref, xn_scr):
    @pl.when(pl.program_id(2) == 0)
    def _():
        xn_scr[...] = _rms(ckv_ref[...], g_ref[...]).astype(BF16)

    y = jnp.dot(xn_scr[...], w_ref[...], preferred_element_type=F32)
    krb = kr_ref[...]
    kr = krb[:, :MLA_ROPE_DIM] * cos_ref[...] + krb[:, MLA_ROPE_DIM:] * sin_ref[...]
    k_ref[...] = jnp.concatenate([y[:, :MLA_NOPE_DIM], kr], axis=-1).astype(BF16)
    v_ref[...] = y[:, MLA_NOPE_DIM:].astype(BF16)


def _mla_kv(y_in, ckv_col, kr_col, g, w, cos, sin, nb, sl):
    tm = ROW_TILE
    return pl.pallas_call(
        _mla_kv_kernel,
        grid=(nb, sl // tm, MLA_HEADS),
        in_specs=[pl.BlockSpec((tm, MLA_KV_RANK), lambda b, i, h: (b * (sl // tm) + i, ckv_col)),
                  pl.BlockSpec((tm, LANES), lambda b, i, h: (b * (sl // tm) + i, kr_col)),
                  pl.BlockSpec((1, MLA_KV_RANK), lambda b, i, h: (0, 0)),
                  pl.BlockSpec((None, MLA_KV_RANK, 2 * LANES), lambda b, i, h: (h, 0, 0)),
                  pl.BlockSpec((tm, MLA_ROPE_DIM), lambda b, i, h: (i, 0)),
                  pl.BlockSpec((tm, MLA_ROPE_DIM), lambda b, i, h: (i, 0))],
        out_specs=[pl.BlockSpec((None, None, tm, MLA_QK_DIM), lambda b, i, h: (b, h, i, 0)),
                   pl.BlockSpec((None, None, tm, MLA_V_DIM), lambda b, i, h: (b, h, i, 0))],
        out_shape=[jax.ShapeDtypeStruct((nb, MLA_HEADS, sl, MLA_QK_DIM), BF16),
                   jax.ShapeDtypeStruct((nb, MLA_HEADS, sl, MLA_V_DIM), BF16)],
        scratch_shapes=[pltpu.VMEM((tm, MLA_KV_RANK), BF16)],
        compiler_params=_cparams("arbitrary", "arbitrary", "arbitrary"),
        name="mla_key_value_prep",
    )(y_in, y_in, g.reshape(1, -1), w, cos, sin)


def _head_kernel(x_ref, g_ref, cos_ref, sin_ref, o_ref, *, norm, rope):
    x = x_ref[...]
    if norm:
        x = _rms(x, g_ref[...])
    if rope:
        q = GQA_HEAD_DIM // 4
        lane = lax.broadcasted_iota(jnp.int32, x.shape, 1)
        first = (lane % (2 * q)) < q
        rot = jnp.where(first, -pltpu.roll(x, GQA_HEAD_DIM - q, 1), pltpu.roll(x, q, 1))
        x = x * cos_ref[...] + rot * sin_ref[...]
    o_ref[...] = x.astype(BF16)


def _head_prep(y_in, col0, n_heads, g, cos, sin, nb, sl, norm, rope):
    tm = ROW_TILE
    return pl.pallas_call(
        functools.partial(_head_kernel, norm=norm, rope=rope),
        grid=(nb, sl // tm, n_heads),
        in_specs=[pl.BlockSpec((tm, GQA_HEAD_DIM), lambda b, i, h: (b * (sl // tm) + i, col0 + h)),
                  pl.BlockSpec((1, GQA_HEAD_DIM), lambda b, i, h: (0, 0)),
                  pl.BlockSpec((tm, GQA_HEAD_DIM), lambda b, i, h: (i, 0)),
                  pl.BlockSpec((tm, GQA_HEAD_DIM), lambda b, i, h: (i, 0))],
        out_specs=pl.BlockSpec((None, None, tm, GQA_HEAD_DIM), lambda b, i, h: (b, h, i, 0)),
        out_shape=jax.ShapeDtypeStruct((nb, n_heads, sl, GQA_HEAD_DIM), BF16),
        compiler_params=_cparams("arbitrary", "arbitrary", "arbitrary"),
        name="gqa_head_prep",
    )(y_in, g.reshape(1, -1), cos, sin)


def _attn_kernel(q_ref, kc_ref, kl_ref, vc_ref, vl_ref, o_ref, *, scale):
    q = q_ref[...]
    dn = (((1,), (1,)), ((), ()))
    sc = lax.dot_general(q, kc_ref[...], dn, preferred_element_type=F32) * scale
    sl = lax.dot_general(q, kl_ref[...], dn, preferred_element_type=F32) * scale
    m = jnp.maximum(jnp.max(sc, axis=-1, keepdims=True), jnp.max(sl, axis=-1, keepdims=True))
    pc = jnp.exp(sc - m)
    pl_ = jnp.exp(sl - m)
    denom = jnp.sum(pc, axis=-1, keepdims=True) + jnp.sum(pl_, axis=-1, keepdims=True)
    o = jnp.dot(pc.astype(BF16), vc_ref[...], preferred_element_type=F32)
    o = o + jnp.dot(pl_.astype(BF16), vl_ref[...], preferred_element_type=F32)
    o_ref[...] = (o / denom).astype(BF16)


def _attention(q, kc, kl, vc, vl, scale):
    nb, nh, sl, dk = q.shape
    hk, tc, tl, dv = kc.shape[1], kc.shape[2], kl.shape[2], vc.shape[3]
    grp = nh // hk
    tq = ATTN_Q_TILE
    kv_spec = lambda t, d: pl.BlockSpec((None, None, t, d), lambda b, h, i: (b, h // grp, 0, 0))
    return pl.pallas_call(
        functools.partial(_attn_kernel, scale=scale),
        grid=(nb, nh, sl // tq),
        in_specs=[pl.BlockSpec((None, None, tq, dk), lambda b, h, i: (b, h, i, 0)),
                  kv_spec(tc, dk), kv_spec(tl, dk), kv_spec(tc, dv), kv_spec(tl, dv)],
        out_specs=pl.BlockSpec((tq, dv), lambda b, h, i: (b * (sl // tq) + i, h)),
        out_shape=jax.ShapeDtypeStruct((nb * sl, nh * dv), BF16),
        compiler_params=_cparams("arbitrary", "arbitrary", "arbitrary"),
        name="softmax_attention",
    )(q, kc, kl, vc, vl)


def _conv_kernel(a_ref, ag_ref, b_ref, c_ref, u_ref,
                 ap_ref, agp_ref, cp_ref, up_ref, an_ref, agn_ref, cn_ref, un_ref,
                 cw_ref, cb_ref, lg_ref, lb_ref, sw_ref, o_ref, ext_scr, ext2_scr):
    i = pl.program_id(1)
    n = pl.num_programs(1)
    ts = a_ref.shape[0]
    hl = CONV_HALO
    has_prev = (i > 0).astype(F32)
    has_next = (i + 1 < n).astype(F32)
    glu = lambda a, g: a * jax.nn.sigmoid(g)
    ext_scr[pl.ds(0, hl), :] = glu(ap_ref[...], agp_ref[...]) * has_prev
    ext_scr[pl.ds(hl, ts), :] = glu(a_ref[...], ag_ref[...])
    ext_scr[pl.ds(hl + ts, hl), :] = glu(an_ref[...], agn_ref[...]) * has_next
    ext2_scr[pl.ds(0, hl), :] = cp_ref[...] * up_ref[...] * has_prev
    ext2_scr[pl.ds(hl, ts), :] = c_ref[...] * u_ref[...]
    ext2_scr[pl.ds(hl + ts, hl), :] = cn_ref[...] * un_ref[...] * has_next

    pad = (CFM_KERNEL - 1) // 2
    acc = cw_ref[pl.ds(0, 1), :] * ext_scr[pl.ds(hl - pad, ts), :]
    for k in range(1, CFM_KERNEL):
        acc = acc + cw_ref[pl.ds(k, 1), :] * ext_scr[pl.ds(hl - pad + k, ts), :]
    acc = acc + cb_ref[...]
    mu = jnp.mean(acc, axis=-1, keepdims=True)
    xc = acc - mu
    yn = xc * lax.rsqrt(jnp.mean(xc * xc, axis=-1, keepdims=True) + NORM_EPS) * lg_ref[...] + lb_ref[...]
    yc = yn * jax.nn.sigmoid(yn)

    pad2 = (SC_KERNEL - 1) // 2
    acc2 = sw_ref[pl.ds(0, 1), :] * ext2_scr[pl.ds(hl - pad2, ts), :]
    for k in range(1, SC_KERNEL):
        acc2 = acc2 + sw_ref[pl.ds(k, 1), :] * ext2_scr[pl.ds(hl - pad2 + k, ts), :]
    yd = b_ref[...] * acc2
    o_ref[...] = jnp.concatenate([yc, yd], axis=-1).astype(BF16)


def _conv_mixer(y_in, cfm_w, cfm_b, ln_g, ln_b, sc_w, nb, sl):
    ts, hl, cw = CONV_TILE, CONV_HALO, CFM_WIDTH
    nt = sl // ts
    rb = ts // hl
    last = nb * sl // hl - 1
    cur = lambda col: pl.BlockSpec((ts, cw), lambda b, i: (b * nt + i, col))
    prev = lambda col: pl.BlockSpec((hl, cw), lambda b, i: (jnp.maximum((b * nt + i) * rb - 1, 0), col))
    nxt = lambda col: pl.BlockSpec((hl, cw), lambda b, i: (jnp.minimum((b * nt + i + 1) * rb, last), col))
    vec = lambda r: pl.BlockSpec((r, cw), lambda b, i: (0, 0))
    return pl.pallas_call(
        _conv_kernel,
        grid=(nb, nt),
        in_specs=[cur(0), cur(1), cur(2), cur(3), cur(4),
                  prev(0), prev(1), prev(3), prev(4), nxt(0), nxt(1), nxt(3), nxt(4),
                  vec(CFM_KERNEL), vec(1), vec(1), vec(1), vec(SC_KERNEL)],
        out_specs=pl.BlockSpec((ts, 2 * cw), lambda b, i: (b * nt + i, 0)),
        out_shape=jax.ShapeDtypeStruct((nb * sl, 2 * cw), BF16),
        scratch_shapes=[pltpu.VMEM((ts + 2 * hl, cw), F32), pltpu.VMEM((ts + 2 * hl, cw), F32)],
        compiler_params=_cparams("arbitrary", "arbitrary"),
        name="conv_mixer",
    )(*([y_in] * 13), cfm_w, cfm_b.reshape(1, cw), ln_g.reshape(1, cw), ln_b.reshape(1, cw), sc_w)


_NO_RANK = 1 << 20


def _top_k_rows(s, rank=None, payload=None):
    if rank is None:
        rank = lax.broadcasted_iota(jnp.int32, s.shape, 0)
    vals, idxs = [], []
    for _ in range(PEER_TOPK):
        m = jnp.max(s, axis=0, keepdims=True)
        idx = jnp.min(jnp.where(s == m, rank, _NO_RANK), axis=0, keepdims=True)
        hit = rank == idx
        vals.append(m)
        idxs.append(idx if payload is None else jnp.sum(jnp.where(hit, payload, 0), axis=0, keepdims=True))
        s = jnp.where(hit, -jnp.inf, s)
    return jnp.concatenate(vals, axis=0), jnp.concatenate(idxs, axis=0)


def _pair_candidates(v1, i1, v2, i2):
    k = PEER_TOPK
    r = lax.broadcasted_iota(jnp.int32, (SUBLANES, v1.shape[1]), 0)
    lo, hi = slice(0, SUBLANES), slice(SUBLANES, 2 * SUBLANES)
    groups = []

    def row_a(a, b_rows, n_valid):
        b0 = b_rows.start
        groups.append((v1[a:a + 1] + v2[b_rows], i1[a:a + 1] * PEER_N_KEYS + i2[b_rows], a * k + b0 + r,
                       None if n_valid == SUBLANES else r < n_valid))

    def col_b(b, a_rows, valid):
        a0 = a_rows.start
        groups.append((v1[a_rows] + v2[b:b + 1], i1[a_rows] * PEER_N_KEYS + i2[b:b + 1], (a0 + r) * k + b, valid))

    row_a(0, lo, 8)
    row_a(0, hi, 8)
    row_a(1, lo, 8)
    col_b(0, hi, None)
    row_a(2, lo, k // 3)
    row_a(3, lo, k // 4)
    row_a(4, lo, k // 5)
    col_b(0, lo, r >= 5)
    col_b(1, lo, r >= 5)
    vals = jnp.concatenate([v if ok is None else jnp.where(ok, v, -jnp.inf) for v, _, _, ok in groups], axis=0)
    flat = jnp.concatenate([f if ok is None else jnp.where(ok, f, _NO_RANK) for _, _, f, ok in groups], axis=0)
    ids = jnp.concatenate([c if ok is None else jnp.where(ok, c, 0) for _, c, _, ok in groups], axis=0)
    return vals, flat, ids


def _route_kernel(q_ref, k1_ref, k2_ref, ids_ref, gate_ref):
    half = PEER_KEY_DIM // 2
    q = q_ref[...].astype(BF16)
    dn = (((1,), (1,)), ((), ()))
    s1 = lax.dot_general(k1_ref[...], q[:, :half], dn, preferred_element_type=F32)
    s2 = lax.dot_general(k2_ref[...], q[:, half:], dn, preferred_element_type=F32)
    v1, i1 = _top_k_rows(s1)
    v2, i2 = _top_k_rows(s2)
    cand, flat, cand_id = _pair_candidates(v1, i1, v2, i2)
    best, ids = _top_k_rows(cand, flat, cand_id)
    e = jnp.exp(best - jnp.max(best, axis=0, keepdims=True))
    gate_ref[...] = e / jnp.sum(e, axis=0, keepdims=True)
    ids_ref[...] = ids


def _peer_route(q, key1, key2):
    m = q.shape[0]
    tm = ROW_TILE
    out_spec = pl.BlockSpec((PEER_TOPK, tm), lambda i, h: (h, i))
    return pl.pallas_call(
        _route_kernel,
        grid=(m // tm, PEER_HEADS),
        in_specs=[pl.BlockSpec((tm, PEER_KEY_DIM), lambda i, h: (i, h)),
                  pl.BlockSpec(key1.shape, lambda i, h: (0, 0)),
                  pl.BlockSpec(key2.shape, lambda i, h: (0, 0))],
        out_specs=[out_spec, out_spec],
        out_shape=[jax.ShapeDtypeStruct((PEER_PICKS, m), jnp.int32),
                   jax.ShapeDtypeStruct((PEER_PICKS, m), F32)],
        compiler_params=_cparams("arbitrary", "arbitrary"),
        name="peer_route",
    )(q, key1, key2)


def _gather_rows(tt):
    return tt * PEER_PICKS


def _issue_rows(ids_ref, tab_ref, buf_ref, sem, slot, rows, row0, n_rows, active):
    groups = rows // SUBLANES
    unroll = 2 * SUBLANES

    def body(g, carry):
        base = pl.multiple_of(row0 + g * unroll, unroll)
        blk = buf_ref.at[pl.ds(slot * groups + row0 // SUBLANES + g * 2, 2)]
        for k in range(unroll):
            pltpu.make_async_copy(tab_ref.at[ids_ref[base + k]], blk.at[k // SUBLANES, pl.ds(k % SUBLANES, 1)],
                                  sem.at[slot]).start(priority=k % 2)
        return carry

    lax.fori_loop(0, active * (n_rows // unroll), body, 0)


def _wait_rows(buf_ref, sem, slot, rows):
    groups = rows // SUBLANES
    half = buf_ref.at[pl.ds(slot * groups, groups)]
    pltpu.make_async_copy(half, half, sem.at[slot]).wait()


def _gather_kernel(ids_cur, ids_nxt, h_ref, gate_ref, x_ref, mg_ref, fg_ref, uv_ref, o_ref, buf, sem, *, final_norm):
    i = pl.program_id(0)
    n = pl.num_programs(0)
    tt = h_ref.shape[0]
    rows = _gather_rows(tt)
    slot = i % 2

    @pl.when(i == 0)
    def _():
        _issue_rows(ids_cur, uv_ref, buf, sem, 0, rows, 0, rows, 1)

    _wait_rows(buf, sem, slot, rows)
    has_next = (i + 1 < n).astype(jnp.int32)
    pg = PEER_PICKS // SUBLANES
    d = h_ref.shape[1]
    groups = rows // SUBLANES
    _issue_rows(ids_nxt, uv_ref, buf, sem, 1 - slot, rows, 0, rows, has_next)
    for t in range(tt):
        w = buf[pl.ds(slot * groups + t * pg, pg)].reshape(PEER_PICKS, d)
        ut = lax.bitcast_convert_type(w & jnp.uint32(0xFFFF0000), F32)
        vt = lax.bitcast_convert_type(w << 16, F32)
        s = jnp.sum(ut * h_ref[pl.ds(t, 1), :], axis=-1, keepdims=True)
        a = jax.nn.gelu(s) * gate_ref[:, pl.ds(t, 1)]
        y = jnp.sum(vt * a, axis=0, keepdims=True)
        o_ref[pl.ds(t, 1), :] = x_ref[pl.ds(t, 1), :] + mg_ref[...] * y
    if final_norm:
        o_ref[...] = _rms(o_ref[...], fg_ref[...])


def _pack_expert_tables(u, v):
    row = (u.shape[0], 1, u.shape[1])
    hi = lax.bitcast_convert_type(u.astype(BF16), jnp.uint16).astype(jnp.uint32).reshape(row)
    lo = lax.bitcast_convert_type(v.astype(BF16), jnp.uint16).astype(jnp.uint32).reshape(row)
    return (hi << 16) | lo


def _peer_mix(h2, ids_t, gate_t, x, mod_gate, row_fn, final_g, uv, final_norm):
    m, d = h2.shape
    tt = GATHER_TOKENS
    steps = m // tt
    rows = _gather_rows(tt)
    ids = ids_t.T.reshape(m * PEER_PICKS)
    gate = gate_t.reshape(PEER_PICKS, steps, tt).transpose(1, 0, 2)
    ids_spec = lambda f: pl.BlockSpec((rows,), f, memory_space=pltpu.SMEM)
    tok_spec = pl.BlockSpec((tt, d), lambda i: (i, 0))
    return pl.pallas_call(
        functools.partial(_gather_kernel, final_norm=final_norm),
        grid=(steps,),
        in_specs=[ids_spec(lambda i: (i,)),
                  ids_spec(lambda i: (jnp.minimum(i + 1, steps - 1),)),
                  tok_spec,
                  pl.BlockSpec((None, PEER_PICKS, tt), lambda i: (i, 0, 0)),
                  tok_spec,
                  pl.BlockSpec((None, 1, d), lambda i: (row_fn(i), 0, 0)),
                  pl.BlockSpec((1, d), lambda i: (0, 0)),
                  pl.BlockSpec(memory_space=pl.ANY)],
        out_specs=tok_spec,
        out_shape=jax.ShapeDtypeStruct((m, d), F32),
        scratch_shapes=[pltpu.VMEM((2 * rows // SUBLANES, SUBLANES, d), jnp.uint32), pltpu.SemaphoreType.DMA((2,))],
        compiler_params=_cparams("arbitrary"),
        name="peer_gather_mix",
    )(ids, ids, h2, gate, x, mod_gate, final_g.reshape(1, d), uv)


def _rope_tables(n_rows, d_rot):
    row = jnp.broadcast_to(jnp.arange(n_rows, dtype=F32)[:, None], (n_rows, GRID_W)).reshape(-1)
    col = jnp.broadcast_to(jnp.arange(GRID_W, dtype=F32)[None, :], (n_rows, GRID_W)).reshape(-1)
    quarter = d_rot // 4
    inv_freq = ROPE_THETA ** (-jnp.arange(quarter, dtype=F32) / quarter)
    ar = row[:, None] * inv_freq
    ac = col[:, None] * inv_freq
    ang = jnp.concatenate([ar, ar, ac, ac], axis=-1)
    return jnp.cos(ang), jnp.sin(ang)


def _rotate_half_cols(w):
    r1, r2, c1, c2 = jnp.split(w, 4, axis=-1)
    return jnp.concatenate([-r2, r1, -c2, c1], axis=-1)


def _peer(x2, mod, tile_row_fn, gather_row_fn, norm_g, w_q, key1, key2, u, v, final_g, final_norm):
    d = x2.shape[1]
    q, h2 = _norm_mm(x2, norm_g, mod[3], mod[4], tile_row_fn, w_q.astype(BF16), tn=d, emit_h=True)
    ids_t, gate_t = _peer_route(q, key1.astype(BF16), key2.astype(BF16))
    return _peer_mix(h2, ids_t, gate_t, x2, mod[5], gather_row_fn, final_g, _pack_expert_tables(u, v), final_norm)


def kernel(x, c, ctx, c_ctx, ada_w, ada_b, norm1_g, norm2_g, attn_w_in, mla_q_norm_g, mla_w_uq, mla_kv_norm_g, mla_w_ukv, gqa_q_norm_g, gqa_k_norm_g, attn_w_out, conv_w_in, cfm_dw_w, cfm_dw_b, cfm_ln_g, cfm_ln_b, sc_dw_w, conv_w_out, peer_w_q, peer_key1, peer_key2, peer_u, peer_v, final_norm_g):
    nb, sl, d = x.shape
    tc = ctx.shape[1]
    depth = ada_w.shape[0]
    assert sl % ROW_TILE == 0 and tc % ROW_TILE == 0 and sl % GRID_W == 0 and d == 2048

    x2 = x.reshape(nb * sl, d)
    ctx2 = ctx.reshape(nb * tc, d)
    ctx_row = nb
    cc = jnp.zeros((SUBLANES, d), F32).at[:nb].set(c).at[ctx_row].set(c_ctx)
    lat_row = lambda i: (i * ROW_TILE) // sl
    ctx_row_fn = lambda i: ctx_row
    gat_row = lambda i: (i * GATHER_TOKENS) // sl

    cos_m, sin_m = _rope_tables(sl // GRID_W, MLA_ROPE_DIM)
    cos_g, sin_g = _rope_tables(sl // GRID_W, GQA_HEAD_DIM)
    one = lambda n, w: jnp.ones((n, w), F32)
    zero = lambda n, w: jnp.zeros((n, w), F32)

    for i in range(depth):
        mod = _modulation(cc, ada_w[i], ada_b[i]).reshape(SUBLANES, 6, 1, d).transpose(1, 0, 2, 3)
        if i % 2 == 0:
            a = i // 2
            w_in = attn_w_in[a]
            o_cq, o_ckv = MLA_Q_RANK, MLA_Q_RANK + MLA_KV_RANK
            o_kr = o_ckv + MLA_ROPE_DIM
            o_gk = o_kr + GQA_HEADS * GQA_HEAD_DIM
            o_gv = o_gk + GQA_KV_HEADS * GQA_HEAD_DIM
            w_kr = w_in[:, o_ckv:o_kr]
            w_in_r = jnp.concatenate([w_in[:, :o_ckv], w_in[:, o_kr:], w_kr, _rotate_half_cols(w_kr)], axis=1).astype(BF16)
            c_ckv = MLA_Q_RANK // MLA_KV_RANK
            c_gq = o_ckv // LANES
            c_gk = c_gq + GQA_HEADS
            c_gv = c_gk + GQA_KV_HEADS
            c_kr = c_gv + GQA_KV_HEADS
            wq = mla_w_uq[a].reshape(MLA_Q_RANK, MLA_HEADS, MLA_QK_DIM)
            wq = jnp.concatenate([wq, _rotate_half_cols(wq[..., MLA_NOPE_DIM:])], axis=-1).transpose(1, 0, 2).astype(BF16)
            wkv = mla_w_ukv[a].reshape(MLA_KV_RANK, MLA_HEADS, MLA_NOPE_DIM + MLA_V_DIM).transpose(1, 0, 2).astype(BF16)

            y_l = _norm_mm(x2, norm1_g[i], mod[0], mod[1], lat_row, w_in_r, tn=w_in_r.shape[1])
            y_c = _norm_mm(ctx2, norm1_g[i], mod[0], mod[1], ctx_row_fn, w_in_r, tn=w_in_r.shape[1])

            mq = _mla_q(y_l, mla_q_norm_g[a], wq, cos_m, sin_m, nb, sl)
            mk, mv = _mla_kv(y_l, c_ckv, c_kr, mla_kv_norm_g[a], wkv, cos_m, sin_m, nb, sl)
            mkc, mvc = _mla_kv(y_c, c_ckv, c_kr, mla_kv_norm_g[a], wkv, one(tc, MLA_ROPE_DIM), zero(tc, MLA_ROPE_DIM), nb, tc)
            hd = GQA_HEAD_DIM
            gq = _head_prep(y_l, c_gq, GQA_HEADS, gqa_q_norm_g[a], cos_g, sin_g, nb, sl, True, True)
            gk = _head_prep(y_l, c_gk, GQA_KV_HEADS, gqa_k_norm_g[a], cos_g, sin_g, nb, sl, True, True)
            gv = _head_prep(y_l, c_gv, GQA_KV_HEADS, gqa_k_norm_g[a], cos_g, sin_g, nb, sl, False, False)
            gkc = _head_prep(y_c, c_gk, GQA_KV_HEADS, gqa_k_norm_g[a], one(tc, hd), zero(tc, hd), nb, tc, True, False)
            gvc = _head_prep(y_c, c_gv, GQA_KV_HEADS, gqa_k_norm_g[a], one(tc, hd), zero(tc, hd), nb, tc, False, False)

            o_m = _attention(mq, mkc, mk, mvc, mv, MLA_QK_DIM ** -0.5)
            o_g = _attention(gq, gkc, gk, gvc, gv, GQA_HEAD_DIM ** -0.5)
            w_out = attn_w_out[a].astype(BF16)
            n_m = MLA_HEADS * MLA_V_DIM
            x2 = _mm_residual([o_m, o_g], [w_out[:n_m], w_out[n_m:]], x2, mod[2], lat_row)
        else:
            mi = i // 2
            y = _norm_mm(x2, norm1_g[i], mod[0], mod[1], lat_row, conv_w_in[mi].astype(BF16), tn=1280)
            yc = _conv_mixer(y, cfm_dw_w[mi], cfm_dw_b[mi], cfm_ln_g[mi], cfm_ln_b[mi], sc_dw_w[mi], nb, sl)
            x2 = _mm_residual([yc], [conv_w_out[mi].astype(BF16)], x2, mod[2], lat_row)
        x2 = _peer(x2, mod, lat_row, gat_row, norm2_g[i], peer_w_q[i], peer_key1[i], peer_key2[i], peer_u[i], peer_v[i],
                   final_norm_g, final_norm=(i == depth - 1))
    return x2.reshape(nb, sl, d)
```

```python
import functools

import jax
import jax.numpy as jnp
from jax import lax
from jax.experimental import pallas as pl
from jax.experimental.pallas import tpu as pltpu

F32 = jnp.float32
BF16 = jnp.bfloat16

GRID_W = 64
ROPE_THETA = 10000.0
NORM_EPS = 1e-6
MLA_HEADS = 8
MLA_Q_RANK = 512
MLA_KV_RANK = 256
MLA_NOPE_DIM = 128
MLA_ROPE_DIM = 64
MLA_V_DIM = 128
MLA_QK_DIM = MLA_NOPE_DIM + MLA_ROPE_DIM
GQA_HEADS = 8
GQA_KV_HEADS = 2
GQA_HEAD_DIM = 128
CFM_WIDTH = 1024
CFM_KERNEL = 31
SC_WIDTH = 1024
SC_KERNEL = 3
PEER_HEADS = 8
PEER_KEY_DIM = 256
PEER_N_KEYS = 128
PEER_TOPK = 16
PEER_PICKS = PEER_HEADS * PEER_TOPK

LANES = 128
SUBLANES = 8
VMEM_LIMIT = 48 * 1024 * 1024

ROW_TILE = 256
ATTN_Q_TILE = 256
CONV_TILE = 256
CONV_HALO = 16
GATHER_TOKENS = 8


def _cparams(*sem):
    return pltpu.CompilerParams(dimension_semantics=sem, vmem_limit_bytes=VMEM_LIMIT)


def _mod_kernel(c_ref, w_ref, b_ref, o_ref):
    c = c_ref[...]
    a = (c * jax.nn.sigmoid(c)).astype(BF16)
    o_ref[...] = jnp.dot(a, w_ref[...].astype(BF16), preferred_element_type=F32) + b_ref[...]


def _modulation(cc, w, b):
    r, d = cc.shape
    n = w.shape[1]
    tn = 1024
    return pl.pallas_call(
        _mod_kernel,
        grid=(n // tn,),
        in_specs=[pl.BlockSpec((r, d), lambda j: (0, 0)),
                  pl.BlockSpec((d, tn), lambda j: (0, j)),
                  pl.BlockSpec((1, tn), lambda j: (0, j))],
        out_specs=pl.BlockSpec((r, tn), lambda j: (0, j)),
        out_shape=jax.ShapeDtypeStruct((r, n), F32),
        compiler_params=_cparams("arbitrary"),
        name="adaln_modulation",
    )(cc, w, b.reshape(1, n))


def _norm_mm_kernel(x_ref, g_ref, sh_ref, sc_ref, w_ref, o_ref, *rest, emit_h):
    h_scr = rest[-1]

    @pl.when(pl.program_id(1) == 0)
    def _():
        x = x_ref[...]
        y = x * lax.rsqrt(jnp.mean(x * x, axis=-1, keepdims=True) + NORM_EPS) * g_ref[...]
        h = y * (1 + sc_ref[...]) + sh_ref[...]
        h_scr[...] = h.astype(BF16)
        if emit_h:
            rest[0][...] = h

    o_ref[...] = jnp.dot(h_scr[...], w_ref[...], preferred_element_type=F32)


def _norm_mm(x, g, shift, scale, row_fn, w, tn, emit_h=False):
    m, d = x.shape
    n = w.shape[1]
    tm = ROW_TILE
    mod_spec = pl.BlockSpec((None, 1, d), lambda i, j: (row_fn(i), 0, 0))
    out_shape = [jax.ShapeDtypeStruct((m, n), F32)]
    out_specs = [pl.BlockSpec((tm, tn), lambda i, j: (i, j))]
    if emit_h:
        out_shape.append(jax.ShapeDtypeStruct((m, d), F32))
        out_specs.append(pl.BlockSpec((tm, d), lambda i, j: (i, 0)))
    res = pl.pallas_call(
        functools.partial(_norm_mm_kernel, emit_h=emit_h),
        grid=(m // tm, n // tn),
        in_specs=[pl.BlockSpec((tm, d), lambda i, j: (i, 0)),
                  pl.BlockSpec((1, d), lambda i, j: (0, 0)),
                  mod_spec, mod_spec,
                  pl.BlockSpec((d, tn), lambda i, j: (0, j))],
        out_specs=out_specs,
        out_shape=out_shape,
        scratch_shapes=[pltpu.VMEM((tm, d), BF16)],
        compiler_params=_cparams("arbitrary", "arbitrary"),
        name="norm_modulate_matmul",
    )(x, g.reshape(1, d), shift, scale, w)
    return res if emit_h else res[0]


def _mm_res_kernel(*refs, n_a):
    a_refs, w_refs = refs[:n_a], refs[n_a:2 * n_a]
    x_ref, gate_ref, o_ref = refs[2 * n_a:]
    acc = jnp.dot(a_refs[0][...], w_refs[0][...], preferred_element_type=F32)
    for a_ref, w_ref in zip(a_refs[1:], w_refs[1:]):
        acc = acc + jnp.dot(a_ref[...], w_ref[...], preferred_element_type=F32)
    o_ref[...] = x_ref[...] + gate_ref[...] * acc


def _mm_residual(a_list, w_list, x, gate, row_fn):
    m, d = x.shape
    tm = ROW_TILE
    n_a = len(a_list)
    in_specs = [pl.BlockSpec((tm, a.shape[1]), lambda i: (i, 0)) for a in a_list]
    in_specs += [pl.BlockSpec(w.shape, lambda i: (0, 0)) for w in w_list]
    in_specs += [pl.BlockSpec((tm, d), lambda i: (i, 0)),
                 pl.BlockSpec((None, 1, d), lambda i: (row_fn(i), 0, 0))]
    return pl.pallas_call(
        functools.partial(_mm_res_kernel, n_a=n_a),
        grid=(m // tm,),
        in_specs=in_specs,
        out_specs=pl.BlockSpec((tm, d), lambda i: (i, 0)),
        out_shape=jax.ShapeDtypeStruct((m, d), F32),
        compiler_params=_cparams("arbitrary"),
        name="matmul_gated_residual",
    )(*a_list, *w_list, x, gate)


def _rms(x, g):
    return x * lax.rsqrt(jnp.mean(x * x, axis=-1, keepdims=True) + NORM_EPS) * g


def _mla_q_kernel(cq_ref, g_ref, w_ref, cos_ref, sin_ref, o_ref, xn_scr):
    @pl.when(pl.program_id(2) == 0)
    def _():
        xn_scr[...] = _rms(cq_ref[...], g_ref[...]).astype(BF16)

    y = jnp.dot(xn_scr[...], w_ref[...], preferred_element_type=F32)
    nope = y[:, :MLA_NOPE_DIM]
    r = y[:, MLA_NOPE_DIM:MLA_QK_DIM]
    rr = y[:, MLA_QK_DIM:]
    o_ref[...] = jnp.concatenate([nope, r * cos_ref[...] + rr * sin_ref[...]], axis=-1).astype(BF16)


def _mla_q(y_in, g, w, cos, sin, nb, sl):
    tm = ROW_TILE
    return pl.pallas_call(
        _mla_q_kernel,
        grid=(nb, sl // tm, MLA_HEADS),
        in_specs=[pl.BlockSpec((tm, MLA_Q_RANK), lambda b, i, h: (b * (sl // tm) + i, 0)),
                  pl.BlockSpec((1, MLA_Q_RANK), lambda b, i, h: (0, 0)),
                  pl.BlockSpec((None, MLA_Q_RANK, 2 * LANES), lambda b, i, h: (h, 0, 0)),
                  pl.BlockSpec((tm, MLA_ROPE_DIM), lambda b, i, h: (i, 0)),
                  pl.BlockSpec((tm, MLA_ROPE_DIM), lambda b, i, h: (i, 0))],
        out_specs=pl.BlockSpec((None, None, tm, MLA_QK_DIM), lambda b, i, h: (b, h, i, 0)),
        out_shape=jax.ShapeDtypeStruct((nb, MLA_HEADS, sl, MLA_QK_DIM), BF16),
        scratch_shapes=[pltpu.VMEM((tm, MLA_Q_RANK), BF16)],
        compiler_params=_cparams("arbitrary", "arbitrary", "arbitrary"),
        name="mla_query_prep",
    )(y_in, g.reshape(1, -1), w, cos, sin)


def _mla_kv_kernel(ckv_ref, kr_ref, g_ref, w_ref, cos_ref, sin_ref, k_ref, v_ref, xn_scr):
    @pl.when(pl.program_id(2) == 0)
    def _():
        xn_scr[...] = _rms(ckv_ref[...], g_ref[...]).astype(BF16)

    y = jnp.dot(xn_scr[...], w_ref[...], preferred_element_type=F32)
    krb = kr_ref[...]
    kr = krb[:, :MLA_ROPE_DIM] * cos_ref[...] + krb[:, MLA_ROPE_DIM:] * sin_ref[...]
    k_ref[...] = jnp.concatenate([y[:, :MLA_NOPE_DIM], kr], axis=-1).astype(BF16)
    v_ref[...] = y[:, MLA_NOPE_DIM:].astype(BF16)


def _mla_kv(y_in, ckv_col, kr_col, g, w, cos, sin, nb, sl):
    tm = ROW_TILE
    return pl.pallas_call(
        _mla_kv_kernel,
        grid=(nb, sl // tm, MLA_HEADS),
        in_specs=[pl.BlockSpec((tm, MLA_KV_RANK), lambda b, i, h: (b * (sl // tm) + i, ckv_col)),
                  pl.BlockSpec((tm, LANES), lambda b, i, h: (b * (sl // tm) + i, kr_col)),
                  pl.BlockSpec((1, MLA_KV_RANK), lambda b, i, h: (0, 0)),
                  pl.BlockSpec((None, MLA_KV_RANK, 2 * LANES), lambda b, i, h: (h, 0, 0)),
                  pl.BlockSpec((tm, MLA_ROPE_DIM), lambda b, i, h: (i, 0)),
                  pl.BlockSpec((tm, MLA_ROPE_DIM), lambda b, i, h: (i, 0))],
        out_specs=[pl.BlockSpec((None, None, tm, MLA_QK_DIM), lambda b, i, h: (b, h, i, 0)),
                   pl.BlockSpec((None, None, tm, MLA_V_DIM), lambda b, i, h: (b, h, i, 0))],
        out_shape=[jax.ShapeDtypeStruct((nb, MLA_HEADS, sl, MLA_QK_DIM), BF16),
                   jax.ShapeDtypeStruct((nb, MLA_HEADS, sl, MLA_V_DIM), BF16)],
        scratch_shapes=[pltpu.VMEM((tm, MLA_KV_RANK), BF16)],
        compiler_params=_cparams("arbitrary", "arbitrary", "arbitrary"),
        name="mla_key_value_prep",
    )(y_in, y_in, g.reshape(1, -1), w, cos, sin)


def _head_kernel(x_ref, g_ref, cos_ref, sin_ref, o_ref, *, norm, rope):
    x = x_ref[...]
    if norm:
        x = _rms(x, g_ref[...])
    if rope:
        q = GQA_HEAD_DIM // 4
        lane = lax.broadcasted_iota(jnp.int32, x.shape, 1)
        first = (lane % (2 * q)) < q
        rot = jnp.where(first, -pltpu.roll(x, GQA_HEAD_DIM - q, 1), pltpu.roll(x, q, 1))
        x = x * cos_ref[...] + rot * sin_ref[...]
    o_ref[...] = x.astype(BF16)


def _head_prep(y_in, col0, n_heads, g, cos, sin, nb, sl, norm, rope):
    tm = ROW_TILE
    return pl.pallas_call(
        functools.partial(_head_kernel, norm=norm, rope=rope),
        grid=(nb, sl // tm, n_heads),
        in_specs=[pl.BlockSpec((tm, GQA_HEAD_DIM), lambda b, i, h: (b * (sl // tm) + i, col0 + h)),
                  pl.BlockSpec((1, GQA_HEAD_DIM), lambda b, i, h: (0, 0)),
                  pl.BlockSpec((tm, GQA_HEAD_DIM), lambda b, i, h: (i, 0)),
                  pl.BlockSpec((tm, GQA_HEAD_DIM), lambda b, i, h: (i, 0))],
        out_specs=pl.BlockSpec((None, None, tm, GQA_HEAD_DIM), lambda b, i, h: (b, h, i, 0)),
        out_shape=jax.ShapeDtypeStruct((nb, n_heads, sl, GQA_HEAD_DIM), BF16),
        compiler_params=_cparams("arbitrary", "arbitrary", "arbitrary"),
        name="gqa_head_prep",
    )(y_in, g.reshape(1, -1), cos, sin)


def _attn_kernel(q_ref, kc_ref, kl_ref, vc_ref, vl_ref, o_ref, *, scale):
    q = q_ref[...]
    dn = (((1,), (1,)), ((), ()))
    sc = lax.dot_general(q, kc_ref[...], dn, preferred_element_type=F32) * scale
    sl = lax.dot_general(q, kl_ref[...], dn, preferred_element_type=F32) * scale
    m = jnp.maximum(jnp.max(sc, axis=-1, keepdims=True), jnp.max(sl, axis=-1, keepdims=True))
    pc = jnp.exp(sc - m)
    pl_ = jnp.exp(sl - m)
    denom = jnp.sum(pc, axis=-1, keepdims=True) + jnp.sum(pl_, axis=-1, keepdims=True)
    o = jnp.dot(pc.astype(BF16), vc_ref[...], preferred_element_type=F32)
    o = o + jnp.dot(pl_.astype(BF16), vl_ref[...], preferred_element_type=F32)
    o_ref[...] = (o / denom).astype(BF16)


def _attention(q, kc, kl, vc, vl, scale):
    nb, nh, sl, dk = q.shape
    hk, tc, tl, dv = kc.shape[1], kc.shape[2], kl.shape[2], vc.shape[3]
    grp = nh // hk
    tq = ATTN_Q_TILE
    kv_spec = lambda t, d: pl.BlockSpec((None, None, t, d), lambda b, h, i: (b, h // grp, 0, 0))
    return pl.pallas_call(
        functools.partial(_attn_kernel, scale=scale),
        grid=(nb, nh, sl // tq),
        in_specs=[pl.BlockSpec((None, None, tq, dk), lambda b, h, i: (b, h, i, 0)),
                  kv_spec(tc, dk), kv_spec(tl, dk), kv_spec(tc, dv), kv_spec(tl, dv)],
        out_specs=pl.BlockSpec((tq, dv), lambda b, h, i: (b * (sl // tq) + i, h)),
        out_shape=jax.ShapeDtypeStruct((nb * sl, nh * dv), BF16),
        compiler_params=_cparams("arbitrary", "arbitrary", "arbitrary"),
        name="softmax_attention",
    )(q, kc, kl, vc, vl)


def _conv_kernel(a_ref, ag_ref, b_ref, c_ref, u_ref,
                 ap_ref, agp_ref, cp_ref, up_ref, an_ref, agn_ref, cn_ref, un_ref,
                 cw_ref, cb_ref, lg_ref, lb_ref, sw_ref, o_ref, ext_scr, ext2_scr):
    i = pl.program_id(1)
    n = pl.num_programs(1)
    ts = a_ref.shape[0]
    hl = CONV_HALO
    has_prev = (i > 0).astype(F32)
    has_next = (i + 1 < n).astype(F32)
    glu = lambda a, g: a * jax.nn.sigmoid(g)
    ext_scr[pl.ds(0, hl), :] = glu(ap_ref[...], agp_ref[...]) * has_prev
    ext_scr[pl.ds(hl, ts), :] = glu(a_ref[...], ag_ref[...])
    ext_scr[pl.ds(hl + ts, hl), :] = glu(an_ref[...], agn_ref[...]) * has_next
    ext2_scr[pl.ds(0, hl), :] = cp_ref[...] * up_ref[...] * has_prev
    ext2_scr[pl.ds(hl, ts), :] = c_ref[...] * u_ref[...]
    ext2_scr[pl.ds(hl + ts, hl), :] = cn_ref[...] * un_ref[...] * has_next

    pad = (CFM_KERNEL - 1) // 2
    acc = cw_ref[pl.ds(0, 1), :] * ext_scr[pl.ds(hl - pad, ts), :]
    for k in range(1, CFM_KERNEL):
        acc = acc + cw_ref[pl.ds(k, 1), :] * ext_scr[pl.ds(hl - pad + k, ts), :]
    acc = acc + cb_ref[...]
    mu = jnp.mean(acc, axis=-1, keepdims=True)
    xc = acc - mu
    yn = xc * lax.rsqrt(jnp.mean(xc * xc, axis=-1, keepdims=True) + NORM_EPS) * lg_ref[...] + lb_ref[...]
    yc = yn * jax.nn.sigmoid(yn)

    pad2 = (SC_KERNEL - 1) // 2
    acc2 = sw_ref[pl.ds(0, 1), :] * ext2_scr[pl.ds(hl - pad2, ts), :]
    for k in range(1, SC_KERNEL):
        acc2 = acc2 + sw_ref[pl.ds(k, 1), :] * ext2_scr[pl.ds(hl - pad2 + k, ts), :]
    yd = b_ref[...] * acc2
    o_ref[...] = jnp.concatenate([yc, yd], axis=-1).astype(BF16)


def _conv_mixer(y_in, cfm_w, cfm_b, ln_g, ln_b, sc_w, nb, sl):
    ts, hl, cw = CONV_TILE, CONV_HALO, CFM_WIDTH
    nt = sl // ts
    rb = ts // hl
    last = nb * sl // hl - 1
    cur = lambda col: pl.BlockSpec((ts, cw), lambda b, i: (b * nt + i, col))
    prev = lambda col: pl.BlockSpec((hl, cw), lambda b, i: (jnp.maximum((b * nt + i) * rb - 1, 0), col))
    nxt = lambda col: pl.BlockSpec((hl, cw), lambda b, i: (jnp.minimum((b * nt + i + 1) * rb, last), col))
    vec = lambda r: pl.BlockSpec((r, cw), lambda b, i: (0, 0))
    return pl.pallas_call(
        _conv_kernel,
        grid=(nb, nt),
        in_specs=[cur(0), cur(1), cur(2), cur(3), cur(4),
                  prev(0), prev(1), prev(3), prev(4), nxt(0), nxt(1), nxt(3), nxt(4),
                  vec(CFM_KERNEL), vec(1), vec(1), vec(1), vec(SC_KERNEL)],
        out_specs=pl.BlockSpec((ts, 2 * cw), lambda b, i: (b * nt + i, 0)),
        out_shape=jax.ShapeDtypeStruct((nb * sl, 2 * cw), BF16),
        scratch_shapes=[pltpu.VMEM((ts + 2 * hl, cw), F32), pltpu.VMEM((ts + 2 * hl, cw), F32)],
        compiler_params=_cparams("arbitrary", "arbitrary"),
        name="conv_mixer",
    )(*([y_in] * 13), cfm_w, cfm_b.reshape(1, cw), ln_g.reshape(1, cw), ln_b.reshape(1, cw), sc_w)


_NO_RANK = 1 << 20


def _top_k_rows(s, rank=None, payload=None):
    if rank is None:
        rank = lax.broadcasted_iota(jnp.int32, s.shape, 0)
    vals, idxs = [], []
    for _ in range(PEER_TOPK):
        m = jnp.max(s, axis=0, keepdims=True)
        idx = jnp.min(jnp.where(s == m, rank, _NO_RANK), axis=0, keepdims=True)
        hit = rank == idx
        vals.append(m)
        idxs.append(idx if payload is None else jnp.sum(jnp.where(hit, payload, 0), axis=0, keepdims=True))
        s = jnp.where(hit, -jnp.inf, s)
    return jnp.concatenate(vals, axis=0), jnp.concatenate(idxs, axis=0)


def _pair_candidates(v1, i1, v2, i2):
    k = PEER_TOPK
    r = lax.broadcasted_iota(jnp.int32, (SUBLANES, v1.shape[1]), 0)
    lo, hi = slice(0, SUBLANES), slice(SUBLANES, 2 * SUBLANES)
    groups = []

    def row_a(a, b_rows, n_valid):
        b0 = b_rows.start
        groups.append((v1[a:a + 1] + v2[b_rows], i1[a:a + 1] * PEER_N_KEYS + i2[b_rows], a * k + b0 + r,
                       None if n_valid == SUBLANES else r < n_valid))

    def col_b(b, a_rows, valid):
        a0 = a_rows.start
        groups.append((v1[a_rows] + v2[b:b + 1], i1[a_rows] * PEER_N_KEYS + i2[b:b + 1], (a0 + r) * k + b, valid))

    row_a(0, lo, 8)
    row_a(0, hi, 8)
    row_a(1, lo, 8)
    col_b(0, hi, None)
    row_a(2, lo, k // 3)
    row_a(3, lo, k // 4)
    row_a(4, lo, k // 5)
    col_b(0, lo, r >= 5)
    col_b(1, lo, r >= 5)
    vals = jnp.concatenate([v if ok is None else jnp.where(ok, v, -jnp.inf) for v, _, _, ok in groups], axis=0)
    flat = jnp.concatenate([f if ok is None else jnp.where(ok, f, _NO_RANK) for _, _, f, ok in groups], axis=0)
    ids = jnp.concatenate([c if ok is None else jnp.where(ok, c, 0) for _, c, _, ok in groups], axis=0)
    return vals, flat, ids


def _route_kernel(q_ref, k1_ref, k2_ref, ids_ref, gate_ref):
    half = PEER_KEY_DIM // 2
    q = q_ref[...].astype(BF16)
    dn = (((1,), (1,)), ((), ()))
    s1 = lax.dot_general(k1_ref[...], q[:, :half], dn, preferred_element_type=F32)
    s2 = lax.dot_general(k2_ref[...], q[:, half:], dn, preferred_element_type=F32)
    v1, i1 = _top_k_rows(s1)
    v2, i2 = _top_k_rows(s2)
    cand, flat, cand_id = _pair_candidates(v1, i1, v2, i2)
    best, ids = _top_k_rows(cand, flat, cand_id)
    e = jnp.exp(best - jnp.max(best, axis=0, keepdims=True))
    gate_ref[...] = e / jnp.sum(e, axis=0, keepdims=True)
    ids_ref[...] = ids


def _peer_route(q, key1, key2):
    m = q.shape[0]
    tm = ROW_TILE
    out_spec = pl.BlockSpec((PEER_TOPK, tm), lambda i, h: (h, i))
    return pl.pallas_call(
        _route_kernel,
        grid=(m // tm, PEER_HEADS),
        in_specs=[pl.BlockSpec((tm, PEER_KEY_DIM), lambda i, h: (i, h)),
                  pl.BlockSpec(key1.shape, lambda i, h: (0, 0)),
                  pl.BlockSpec(key2.shape, lambda i, h: (0, 0))],
        out_specs=[out_spec, out_spec],
        out_shape=[jax.ShapeDtypeStruct((PEER_PICKS, m), jnp.int32),
                   jax.ShapeDtypeStruct((PEER_PICKS, m), F32)],
        compiler_params=_cparams("arbitrary", "arbitrary"),
        name="peer_route",
    )(q, key1, key2)


def _gather_rows(tt):
    return tt * PEER_PICKS


def _issue_rows(ids_ref, id0, tab_ref, buf_ref, row0, n_rows, sem):
    for r in range(n_rows):
        g, k = divmod(row0 + r, SUBLANES)
        pltpu.make_async_copy(tab_ref.at[ids_ref[id0 + r]], buf_ref.at[g, pl.ds(k, 1)], sem).start(priority=r % 2)


def _issue_rows_loop(ids_ref, tab_ref, buf_ref, n_rows, sem):
    unroll = 2 * SUBLANES

    def body(g, carry):
        base = pl.multiple_of(g * unroll, unroll)
        blk = buf_ref.at[pl.ds(g * 2, 2)]
        for k in range(unroll):
            pltpu.make_async_copy(tab_ref.at[ids_ref[base + k]], blk.at[k // SUBLANES, pl.ds(k % SUBLANES, 1)],
                                  sem).start(priority=k % 2)
        return carry

    lax.fori_loop(0, n_rows // unroll, body, 0)


def _wait_rows(buf_ref, sem):
    pltpu.make_async_copy(buf_ref, buf_ref, sem).wait()


def _mix_token(buf_ref, row0, t, h_ref, gate_ref, x_ref, mg_ref, o_ref):
    d = h_ref.shape[1]
    w = buf_ref[pl.ds(row0 // SUBLANES, PEER_PICKS // SUBLANES)].reshape(PEER_PICKS, d)
    ut = lax.bitcast_convert_type(w & jnp.uint32(0xFFFF0000), F32)
    vt = lax.bitcast_convert_type(w << 16, F32)
    s = jnp.sum(ut * h_ref[pl.ds(t, 1), :], axis=-1, keepdims=True)
    a = jax.nn.gelu(s) * gate_ref[:, pl.ds(t, 1)]
    y = jnp.sum(vt * a, axis=0, keepdims=True)
    o_ref[pl.ds(t, 1), :] = x_ref[pl.ds(t, 1), :] + mg_ref[...] * y


def _gather_kernel(ids_cur, ids_nxt, h_ref, gate_ref, x_ref, mg_ref, fg_ref, uv_ref, o_ref, buf_a, buf_b, sem, *,
                   final_norm):
    i = pl.program_id(0)
    n = pl.num_programs(0)
    th = h_ref.shape[0] // 2
    rows = _gather_rows(th)
    refs = (h_ref, gate_ref, x_ref, mg_ref, o_ref)

    @pl.when(i == 0)
    def _():
        _issue_rows_loop(ids_cur, uv_ref, buf_a, rows, sem.at[0])

    _wait_rows(buf_a, sem.at[0])
    for t in range(th):
        _issue_rows(ids_cur, rows + t * PEER_PICKS, uv_ref, buf_b, t * PEER_PICKS, PEER_PICKS, sem.at[1])
        _mix_token(buf_a, t * PEER_PICKS, t, *refs)
    _wait_rows(buf_b, sem.at[1])
    for t in range(th):
        _issue_rows(ids_nxt, t * PEER_PICKS, uv_ref, buf_a, t * PEER_PICKS, PEER_PICKS, sem.at[0])
        _mix_token(buf_b, t * PEER_PICKS, th + t, *refs)

    @pl.when(i == n - 1)
    def _():
        _wait_rows(buf_a, sem.at[0])

    if final_norm:
        o_ref[...] = _rms(o_ref[...], fg_ref[...])


def _pack_expert_tables(u, v):
    row = (u.shape[0], 1, u.shape[1])
    hi = lax.bitcast_convert_type(u.astype(BF16), jnp.uint16).astype(jnp.uint32).reshape(row)
    lo = lax.bitcast_convert_type(v.astype(BF16), jnp.uint16).astype(jnp.uint32).reshape(row)
    return (hi << 16) | lo


def _peer_mix(h2, ids_t, gate_t, x, mod_gate, row_fn, final_g, uv, final_norm):
    m, d = h2.shape
    tt = GATHER_TOKENS
    steps = m // tt
    rows = _gather_rows(tt)
    ids = ids_t.T.reshape(m * PEER_PICKS)
    gate = gate_t.reshape(PEER_PICKS, steps, tt).transpose(1, 0, 2)
    ids_spec = lambda f: pl.BlockSpec((rows,), f, memory_space=pltpu.SMEM)
    tok_spec = pl.BlockSpec((tt, d), lambda i: (i, 0))
    return pl.pallas_call(
        functools.partial(_gather_kernel, final_norm=final_norm),
        grid=(steps,),
        in_specs=[ids_spec(lambda i: (i,)),
                  ids_spec(lambda i: (jnp.minimum(i + 1, steps - 1),)),
                  tok_spec,
                  pl.BlockSpec((None, PEER_PICKS, tt), lambda i: (i, 0, 0)),
                  tok_spec,
                  pl.BlockSpec((None, 1, d), lambda i: (row_fn(i), 0, 0)),
                  pl.BlockSpec((1, d), lambda i: (0, 0)),
                  pl.BlockSpec(memory_space=pl.ANY)],
        out_specs=tok_spec,
        out_shape=jax.ShapeDtypeStruct((m, d), F32),
        scratch_shapes=[pltpu.VMEM((rows // (2 * SUBLANES), SUBLANES, d), jnp.uint32),
                        pltpu.VMEM((rows // (2 * SUBLANES), SUBLANES, d), jnp.uint32),
                        pltpu.SemaphoreType.DMA((2,))],
        compiler_params=_cparams("arbitrary"),
        name="peer_gather_mix",
    )(ids, ids, h2, gate, x, mod_gate, final_g.reshape(1, d), uv)


def _rope_tables(n_rows, d_rot):
    row = jnp.broadcast_to(jnp.arange(n_rows, dtype=F32)[:, None], (n_rows, GRID_W)).reshape(-1)
    col = jnp.broadcast_to(jnp.arange(GRID_W, dtype=F32)[None, :], (n_rows, GRID_W)).reshape(-1)
    quarter = d_rot // 4
    inv_freq = ROPE_THETA ** (-jnp.arange(quarter, dtype=F32) / quarter)
    ar = row[:, None] * inv_freq
    ac = col[:, None] * inv_freq
    ang = jnp.concatenate([ar, ar, ac, ac], axis=-1)
    return jnp.cos(ang), jnp.sin(ang)


def _rotate_half_cols(w):
    r1, r2, c1, c2 = jnp.split(w, 4, axis=-1)
    return jnp.concatenate([-r2, r1, -c2, c1], axis=-1)


def _peer(x2, mod, tile_row_fn, gather_row_fn, norm_g, w_q, key1, key2, u, v, final_g, final_norm):
    d = x2.shape[1]
    q, h2 = _norm_mm(x2, norm_g, mod[3], mod[4], tile_row_fn, w_q.astype(BF16), tn=d, emit_h=True)
    ids_t, gate_t = _peer_route(q, key1.astype(BF16), key2.astype(BF16))
    return _peer_mix(h2, ids_t, gate_t, x2, mod[5], gather_row_fn, final_g, _pack_expert_tables(u, v), final_norm)


def kernel(x, c, ctx, c_ctx, ada_w, ada_b, norm1_g, norm2_g, attn_w_in, mla_q_norm_g, mla_w_uq, mla_kv_norm_g, mla_w_ukv, gqa_q_norm_g, gqa_k_norm_g, attn_w_out, conv_w_in, cfm_dw_w, cfm_dw_b, cfm_ln_g, cfm_ln_b, sc_dw_w, conv_w_out, peer_w_q, peer_key1, peer_key2, peer_u, peer_v, final_norm_g):
    nb, sl, d = x.shape
    tc = ctx.shape[1]
    depth = ada_w.shape[0]
    assert sl % ROW_TILE == 0 and tc % ROW_TILE == 0 and sl % GRID_W == 0 and d == 2048

    x2 = x.reshape(nb * sl, d)
    ctx2 = ctx.reshape(nb * tc, d)
    ctx_row = nb
    cc = jnp.zeros((SUBLANES, d), F32).at[:nb].set(c).at[ctx_row].set(c_ctx)
    lat_row = lambda i: (i * ROW_TILE) // sl
    ctx_row_fn = lambda i: ctx_row
    gat_row = lambda i: (i * GATHER_TOKENS) // sl

    cos_m, sin_m = _rope_tables(sl // GRID_W, MLA_ROPE_DIM)
    cos_g, sin_g = _rope_tables(sl // GRID_W, GQA_HEAD_DIM)
    one = lambda n, w: jnp.ones((n, w), F32)
    zero = lambda n, w: jnp.zeros((n, w), F32)

    for i in range(depth):
        mod = _modulation(cc, ada_w[i], ada_b[i]).reshape(SUBLANES, 6, 1, d).transpose(1, 0, 2, 3)
        if i % 2 == 0:
            a = i // 2
            w_in = attn_w_in[a]
            o_cq, o_ckv = MLA_Q_RANK, MLA_Q_RANK + MLA_KV_RANK
            o_kr = o_ckv + MLA_ROPE_DIM
            o_gk = o_kr + GQA_HEADS * GQA_HEAD_DIM
            o_gv = o_gk + GQA_KV_HEADS * GQA_HEAD_DIM
            w_kr = w_in[:, o_ckv:o_kr]
            w_in_r = jnp.concatenate([w_in[:, :o_ckv], w_in[:, o_kr:], w_kr, _rotate_half_cols(w_kr)], axis=1).astype(BF16)
            c_ckv = MLA_Q_RANK // MLA_KV_RANK
            c_gq = o_ckv // LANES
            c_gk = c_gq + GQA_HEADS
            c_gv = c_gk + GQA_KV_HEADS
            c_kr = c_gv + GQA_KV_HEADS
            wq = mla_w_uq[a].reshape(MLA_Q_RANK, MLA_HEADS, MLA_QK_DIM)
            wq = jnp.concatenate([wq, _rotate_half_cols(wq[..., MLA_NOPE_DIM:])], axis=-1).transpose(1, 0, 2).astype(BF16)
            wkv = mla_w_ukv[a].reshape(MLA_KV_RANK, MLA_HEADS, MLA_NOPE_DIM + MLA_V_DIM).transpose(1, 0, 2).astype(BF16)

            y_l = _norm_mm(x2, norm1_g[i], mod[0], mod[1], lat_row, w_in_r, tn=w_in_r.shape[1])
            y_c = _norm_mm(ctx2, norm1_g[i], mod[0], mod[1], ctx_row_fn, w_in_r, tn=w_in_r.shape[1])

            mq = _mla_q(y_l, mla_q_norm_g[a], wq, cos_m, sin_m, nb, sl)
            mk, mv = _mla_kv(y_l, c_ckv, c_kr, mla_kv_norm_g[a], wkv, cos_m, sin_m, nb, sl)
            mkc, mvc = _mla_kv(y_c, c_ckv, c_kr, mla_kv_norm_g[a], wkv, one(tc, MLA_ROPE_DIM), zero(tc, MLA_ROPE_DIM), nb, tc)
            hd = GQA_HEAD_DIM
            gq = _head_prep(y_l, c_gq, GQA_HEADS, gqa_q_norm_g[a], cos_g, sin_g, nb, sl, True, True)
            gk = _head_prep(y_l, c_gk, GQA_KV_HEADS, gqa_k_norm_g[a], cos_g, sin_g, nb, sl, True, True)
            gv = _head_prep(y_l, c_gv, GQA_KV_HEADS, gqa_k_norm_g[a], cos_g, sin_g, nb, sl, False, False)
            gkc = _head_prep(y_c, c_gk, GQA_KV_HEADS, gqa_k_norm_g[a], one(tc, hd), zero(tc, hd), nb, tc, True, False)
            gvc = _head_prep(y_c, c_gv, GQA_KV_HEADS, gqa_k_norm_g[a], one(tc, hd), zero(tc, hd), nb, tc, False, False)

            o_m = _attention(mq, mkc, mk, mvc, mv, MLA_QK_DIM ** -0.5)
            o_g = _attention(gq, gkc, gk, gvc, gv, GQA_HEAD_DIM ** -0.5)
            w_out = attn_w_out[a].astype(BF16)
            n_m = MLA_HEADS * MLA_V_DIM
            x2 = _mm_residual([o_m, o_g], [w_out[:n_m], w_out[n_m:]], x2, mod[2], lat_row)
        else:
            mi = i // 2
            y = _norm_mm(x2, norm1_g[i], mod[0], mod[1], lat_row, conv_w_in[mi].astype(BF16), tn=1280)
            yc = _conv_mixer(y, cfm_dw_w[mi], cfm_dw_b[mi], cfm_ln_g[mi], cfm_ln_b[mi], sc_dw_w[mi], nb, sl)
            x2 = _mm_residual([yc], [conv_w_out[mi].astype(BF16)], x2, mod[2], lat_row)
        x2 = _peer(x2, mod, lat_row, gat_row, norm2_g[i], peer_w_q[i], peer_key1[i], peer_key2[i], peer_u[i], peer_v[i],
                   final_norm_g, final_norm=(i == depth - 1))
    return x2.reshape(nb, sl, d)
```

```python
import functools

import jax
import jax.numpy as jnp
from jax import lax
from jax.experimental import pallas as pl
from jax.experimental.pallas import tpu as pltpu

F32 = jnp.float32
BF16 = jnp.bfloat16

GRID_W = 64
ROPE_THETA = 10000.0
NORM_EPS = 1e-6
MLA_HEADS = 8
MLA_Q_RANK = 512
MLA_KV_RANK = 256
MLA_NOPE_DIM = 128
MLA_ROPE_DIM = 64
MLA_V_DIM = 128
MLA_QK_DIM = MLA_NOPE_DIM + MLA_ROPE_DIM
GQA_HEADS = 8
GQA_KV_HEADS = 2
GQA_HEAD_DIM = 128
CFM_WIDTH = 1024
CFM_KERNEL = 31
SC_WIDTH = 1024
SC_KERNEL = 3
PEER_HEADS = 8
PEER_KEY_DIM = 256
PEER_N_KEYS = 128
PEER_TOPK = 16
PEER_PICKS = PEER_HEADS * PEER_TOPK

LANES = 128
SUBLANES = 8
VMEM_LIMIT = 48 * 1024 * 1024

ROW_TILE = 256
ATTN_Q_TILE = 256
CONV_TILE = 256
CONV_HALO = 16
GATHER_TOKENS = 8


def _cparams(*sem):
    return pltpu.CompilerParams(dimension_semantics=sem, vmem_limit_bytes=VMEM_LIMIT)


def _mod_kernel(c_ref, w_ref, b_ref, o_ref):
    c = c_ref[...]
    a = (c * jax.nn.sigmoid(c)).astype(BF16)
    o_ref[...] = jnp.dot(a, w_ref[...].astype(BF16), preferred_element_type=F32) + b_ref[...]


def _modulation(cc, w, b):
    r, d = cc.shape
    n = w.shape[1]
    tn = 1024
    return pl.pallas_call(
        _mod_kernel,
        grid=(n // tn,),
        in_specs=[pl.BlockSpec((r, d), lambda j: (0, 0)),
                  pl.BlockSpec((d, tn), lambda j: (0, j)),
                  pl.BlockSpec((1, tn), lambda j: (0, j))],
        out_specs=pl.BlockSpec((r, tn), lambda j: (0, j)),
        out_shape=jax.ShapeDtypeStruct((r, n), F32),
        compiler_params=_cparams("arbitrary"),
        name="adaln_modulation",
    )(cc, w, b.reshape(1, n))


def _norm_mm_kernel(x_ref, g_ref, sh_ref, sc_ref, w_ref, o_ref, *rest, emit_h):
    h_scr = rest[-1]

    @pl.when(pl.program_id(1) == 0)
    def _():
        x = x_ref[...]
        y = x * lax.rsqrt(jnp.mean(x * x, axis=-1, keepdims=True) + NORM_EPS) * g_ref[...]
        h = y * (1 + sc_ref[...]) + sh_ref[...]
        h_scr[...] = h.astype(BF16)
        if emit_h:
            rest[0][...] = h

    o_ref[...] = jnp.dot(h_scr[...], w_ref[...], preferred_element_type=F32)


def _norm_mm(x, g, shift, scale, row_fn, w, tn, emit_h=False):
    m, d = x.shape
    n = w.shape[1]
    tm = ROW_TILE
    mod_spec = pl.BlockSpec((None, 1, d), lambda i, j: (row_fn(i), 0, 0))
    out_shape = [jax.ShapeDtypeStruct((m, n), F32)]
    out_specs = [pl.BlockSpec((tm, tn), lambda i, j: (i, j))]
    if emit_h:
        out_shape.append(jax.ShapeDtypeStruct((m, d), F32))
        out_specs.append(pl.BlockSpec((tm, d), lambda i, j: (i, 0)))
    res = pl.pallas_call(
        functools.partial(_norm_mm_kernel, emit_h=emit_h),
        grid=(m // tm, n // tn),
        in_specs=[pl.BlockSpec((tm, d), lambda i, j: (i, 0)),
                  pl.BlockSpec((1, d), lambda i, j: (0, 0)),
                  mod_spec, mod_spec,
                  pl.BlockSpec((d, tn), lambda i, j: (0, j))],
        out_specs=out_specs,
        out_shape=out_shape,
        scratch_shapes=[pltpu.VMEM((tm, d), BF16)],
        compiler_params=_cparams("arbitrary", "arbitrary"),
        name="norm_modulate_matmul",
    )(x, g.reshape(1, d), shift, scale, w)
    return res if emit_h else res[0]


def _mm_res_kernel(*refs, n_a):
    a_refs, w_refs = refs[:n_a], refs[n_a:2 * n_a]
    x_ref, gate_ref, o_ref = refs[2 * n_a:]
    acc = jnp.dot(a_refs[0][...], w_refs[0][...], preferred_element_type=F32)
    for a_ref, w_ref in zip(a_refs[1:], w_refs[1:]):
        acc = acc + jnp.dot(a_ref[...], w_ref[...], preferred_element_type=F32)
    o_ref[...] = x_ref[...] + gate_ref[...] * acc


def _mm_residual(a_list, w_list, x, gate, row_fn):
    m, d = x.shape
    tm = ROW_TILE
    n_a = len(a_list)
    in_specs = [pl.BlockSpec((tm, a.shape[1]), lambda i: (i, 0)) for a in a_list]
    in_specs += [pl.BlockSpec(w.shape, lambda i: (0, 0)) for w in w_list]
    in_specs += [pl.BlockSpec((tm, d), lambda i: (i, 0)),
                 pl.BlockSpec((None, 1, d), lambda i: (row_fn(i), 0, 0))]
    return pl.pallas_call(
        functools.partial(_mm_res_kernel, n_a=n_a),
        grid=(m // tm,),
        in_specs=in_specs,
        out_specs=pl.BlockSpec((tm, d), lambda i: (i, 0)),
        out_shape=jax.ShapeDtypeStruct((m, d), F32),
        compiler_params=_cparams("arbitrary"),
        name="matmul_gated_residual",
    )(*a_list, *w_list, x, gate)


def _rms(x, g):
    return x * lax.rsqrt(jnp.mean(x * x, axis=-1, keepdims=True) + NORM_EPS) * g


def _mla_q_kernel(cq_ref, g_ref, w_ref, cos_ref, sin_ref, o_ref, xn_scr):
    @pl.when(pl.program_id(2) == 0)
    def _():
        xn_scr[...] = _rms(cq_ref[...], g_ref[...]).astype(BF16)

    y = jnp.dot(xn_scr[...], w_ref[...], preferred_element_type=F32)
    nope = y[:, :MLA_NOPE_DIM]
    r = y[:, MLA_NOPE_DIM:MLA_QK_DIM]
    rr = y[:, MLA_QK_DIM:]
    o_ref[...] = jnp.concatenate([nope, r * cos_ref[...] + rr * sin_ref[...]], axis=-1).astype(BF16)


def _mla_q(y_in, g, w, cos, sin, nb, sl):
    tm = ROW_TILE
    return pl.pallas_call(
        _mla_q_kernel,
        grid=(nb, sl // tm, MLA_HEADS),
        in_specs=[pl.BlockSpec((tm, MLA_Q_RANK), lambda b, i, h: (b * (sl // tm) + i, 0)),
                  pl.BlockSpec((1, MLA_Q_RANK), lambda b, i, h: (0, 0)),
                  pl.BlockSpec((None, MLA_Q_RANK, 2 * LANES), lambda b, i, h: (h, 0, 0)),
                  pl.BlockSpec((tm, MLA_ROPE_DIM), lambda b, i, h: (i, 0)),
                  pl.BlockSpec((tm, MLA_ROPE_DIM), lambda b, i, h: (i, 0))],
        out_specs=pl.BlockSpec((None, None, tm, MLA_QK_DIM), lambda b, i, h: (b, h, i, 0)),
        out_shape=jax.ShapeDtypeStruct((nb, MLA_HEADS, sl, MLA_QK_DIM), BF16),
        scratch_shapes=[pltpu.VMEM((tm, MLA_Q_RANK), BF16)],
        compiler_params=_cparams("arbitrary", "arbitrary", "arbitrary"),
        name="mla_query_prep",
    )(y_in, g.reshape(1, -1), w, cos, sin)


def _mla_kv_kernel(ckv_ref, kr_ref, g_ref, w_ref, cos_ref, sin_ref, k_ref, v_ref, xn_scr):
    @pl.when(pl.program_id(2) == 0)
    def _():
        xn_scr[...] = _rms(ckv_ref[...], g_ref[...]).astype(BF16)

    y = jnp.dot(xn_scr[...], w_ref[...], preferred_element_type=F32)
    krb = kr_ref[...]
    kr = krb[:, :MLA_ROPE_DIM] * cos_ref[...] + krb[:, MLA_ROPE_DIM:] * sin_ref[...]
    k_ref[...] = jnp.concatenate([y[:, :MLA_NOPE_DIM], kr], axis=-1).astype(BF16)
    v_ref[...] = y[:, MLA_NOPE_DIM:].astype(BF16)


def _mla_kv(y_in, ckv_col, kr_col, g, w, cos, sin, nb, sl):
    tm = ROW_TILE
    return pl.pallas_call(
        _mla_kv_kernel,
        grid=(nb, sl // tm, MLA_HEADS),
        in_specs=[pl.BlockSpec((tm, MLA_KV_RANK), lambda b, i, h: (b * (sl // tm) + i, ckv_col)),
                  pl.BlockSpec((tm, LANES), lambda b, i, h: (b * (sl // tm) + i, kr_col)),
                  pl.BlockSpec((1, MLA_KV_RANK), lambda b, i, h: (0, 0)),
                  pl.BlockSpec((None, MLA_KV_RANK, 2 * LANES), lambda b, i, h: (h, 0, 0)),
                  pl.BlockSpec((tm, MLA_ROPE_DIM), lambda b, i, h: (i, 0)),
                  pl.BlockSpec((tm, MLA_ROPE_DIM), lambda b, i, h: (i, 0))],
        out_specs=[pl.BlockSpec((None, None, tm, MLA_QK_DIM), lambda b, i, h: (b, h, i, 0)),
                   pl.BlockSpec((None, None, tm, MLA_V_DIM), lambda b, i, h: (b, h, i, 0))],
        out_shape=[jax.ShapeDtypeStruct((nb, MLA_HEADS, sl, MLA_QK_DIM), BF16),
                   jax.ShapeDtypeStruct((nb, MLA_HEADS, sl, MLA_V_DIM), BF16)],
        scratch_shapes=[pltpu.VMEM((tm, MLA_KV_RANK), BF16)],
        compiler_params=_cparams("arbitrary", "arbitrary", "arbitrary"),
        name="mla_key_value_prep",
    )(y_in, y_in, g.reshape(1, -1), w, cos, sin)


def _head_kernel(x_ref, g_ref, cos_ref, sin_ref, o_ref, *, norm, rope):
    x = x_ref[...]
    if norm:
        x = _rms(x, g_ref[...])
    if rope:
        q = GQA_HEAD_DIM // 4
        lane = lax.broadcasted_iota(jnp.int32, x.shape, 1)
        first = (lane % (2 * q)) < q
        rot = jnp.where(first, -pltpu.roll(x, GQA_HEAD_DIM - q, 1), pltpu.roll(x, q, 1))
        x = x * cos_ref[...] + rot * sin_ref[...]
    o_ref[...] = x.astype(BF16)


def _head_prep(y_in, col0, n_heads, g, cos, sin, nb, sl, norm, rope):
    tm = ROW_TILE
    return pl.pallas_call(
        functools.partial(_head_kernel, norm=norm, rope=rope),
        grid=(nb, sl // tm, n_heads),
        in_specs=[pl.BlockSpec((tm, GQA_HEAD_DIM), lambda b, i, h: (b * (sl // tm) + i, col0 + h)),
                  pl.BlockSpec((1, GQA_HEAD_DIM), lambda b, i, h: (0, 0)),
                  pl.BlockSpec((tm, GQA_HEAD_DIM), lambda b, i, h: (i, 0)),
                  pl.BlockSpec((tm, GQA_HEAD_DIM), lambda b, i, h: (i, 0))],
        out_specs=pl.BlockSpec((None, None, tm, GQA_HEAD_DIM), lambda b, i, h: (b, h, i, 0)),
        out_shape=jax.ShapeDtypeStruct((nb, n_heads, sl, GQA_HEAD_DIM), BF16),
        compiler_params=_cparams("arbitrary", "arbitrary", "arbitrary"),
        name="gqa_head_prep",
    )(y_in, g.reshape(1, -1), cos, sin)


def _attn_kernel(q_ref, kc_ref, kl_ref, vc_ref, vl_ref, o_ref, *, scale):
    q = q_ref[...]
    dn = (((1,), (1,)), ((), ()))
    sc = lax.dot_general(q, kc_ref[...], dn, preferred_element_type=F32) * scale
    sl = lax.dot_general(q, kl_ref[...], dn, preferred_element_type=F32) * scale
    m = jnp.maximum(jnp.max(sc, axis=-1, keepdims=True), jnp.max(sl, axis=-1, keepdims=True))
    pc = jnp.exp(sc - m)
    pl_ = jnp.exp(sl - m)
    denom = jnp.sum(pc, axis=-1, keepdims=True) + jnp.sum(pl_, axis=-1, keepdims=True)
    o = jnp.dot(pc.astype(BF16), vc_ref[...], preferred_element_type=F32)
    o = o + jnp.dot(pl_.astype(BF16), vl_ref[...], preferred_element_type=F32)
    o_ref[...] = (o / denom).astype(BF16)


def _attention(q, kc, kl, vc, vl, scale):
    nb, nh, sl, dk = q.shape
    hk, tc, tl, dv = kc.shape[1], kc.shape[2], kl.shape[2], vc.shape[3]
    grp = nh // hk
    tq = ATTN_Q_TILE
    kv_spec = lambda t, d: pl.BlockSpec((None, None, t, d), lambda b, h, i: (b, h // grp, 0, 0))
    return pl.pallas_call(
        functools.partial(_attn_kernel, scale=scale),
        grid=(nb, nh, sl // tq),
        in_specs=[pl.BlockSpec((None, None, tq, dk), lambda b, h, i: (b, h, i, 0)),
                  kv_spec(tc, dk), kv_spec(tl, dk), kv_spec(tc, dv), kv_spec(tl, dv)],
        out_specs=pl.BlockSpec((tq, dv), lambda b, h, i: (b * (sl // tq) + i, h)),
        out_shape=jax.ShapeDtypeStruct((nb * sl, nh * dv), BF16),
        compiler_params=_cparams("arbitrary", "arbitrary", "arbitrary"),
        name="softmax_attention",
    )(q, kc, kl, vc, vl)


def _conv_kernel(a_ref, ag_ref, b_ref, c_ref, u_ref,
                 ap_ref, agp_ref, cp_ref, up_ref, an_ref, agn_ref, cn_ref, un_ref,
                 cw_ref, cb_ref, lg_ref, lb_ref, sw_ref, o_ref, ext_scr, ext2_scr):
    i = pl.program_id(1)
    n = pl.num_programs(1)
    ts = a_ref.shape[0]
    hl = CONV_HALO
    has_prev = (i > 0).astype(F32)
    has_next = (i + 1 < n).astype(F32)
    glu = lambda a, g: a * jax.nn.sigmoid(g)
    ext_scr[pl.ds(0, hl), :] = glu(ap_ref[...], agp_ref[...]) * has_prev
    ext_scr[pl.ds(hl, ts), :] = glu(a_ref[...], ag_ref[...])
    ext_scr[pl.ds(hl + ts, hl), :] = glu(an_ref[...], agn_ref[...]) * has_next
    ext2_scr[pl.ds(0, hl), :] = cp_ref[...] * up_ref[...] * has_prev
    ext2_scr[pl.ds(hl, ts), :] = c_ref[...] * u_ref[...]
    ext2_scr[pl.ds(hl + ts, hl), :] = cn_ref[...] * un_ref[...] * has_next

    pad = (CFM_KERNEL - 1) // 2
    acc = cw_ref[pl.ds(0, 1), :] * ext_scr[pl.ds(hl - pad, ts), :]
    for k in range(1, CFM_KERNEL):
        acc = acc + cw_ref[pl.ds(k, 1), :] * ext_scr[pl.ds(hl - pad + k, ts), :]
    acc = acc + cb_ref[...]
    mu = jnp.mean(acc, axis=-1, keepdims=True)
    xc = acc - mu
    yn = xc * lax.rsqrt(jnp.mean(xc * xc, axis=-1, keepdims=True) + NORM_EPS) * lg_ref[...] + lb_ref[...]
    yc = yn * jax.nn.sigmoid(yn)

    pad2 = (SC_KERNEL - 1) // 2
    acc2 = sw_ref[pl.ds(0, 1), :] * ext2_scr[pl.ds(hl - pad2, ts), :]
    for k in range(1, SC_KERNEL):
        acc2 = acc2 + sw_ref[pl.ds(k, 1), :] * ext2_scr[pl.ds(hl - pad2 + k, ts), :]
    yd = b_ref[...] * acc2
    o_ref[...] = jnp.concatenate([yc, yd], axis=-1).astype(BF16)


def _conv_mixer(y_in, cfm_w, cfm_b, ln_g, ln_b, sc_w, nb, sl):
    ts, hl, cw = CONV_TILE, CONV_HALO, CFM_WIDTH
    nt = sl // ts
    rb = ts // hl
    last = nb * sl // hl - 1
    cur = lambda col: pl.BlockSpec((ts, cw), lambda b, i: (b * nt + i, col))
    prev = lambda col: pl.BlockSpec((hl, cw), lambda b, i: (jnp.maximum((b * nt + i) * rb - 1, 0), col))
    nxt = lambda col: pl.BlockSpec((hl, cw), lambda b, i: (jnp.minimum((b * nt + i + 1) * rb, last), col))
    vec = lambda r: pl.BlockSpec((r, cw), lambda b, i: (0, 0))
    return pl.pallas_call(
        _conv_kernel,
        grid=(nb, nt),
        in_specs=[cur(0), cur(1), cur(2), cur(3), cur(4),
                  prev(0), prev(1), prev(3), prev(4), nxt(0), nxt(1), nxt(3), nxt(4),
                  vec(CFM_KERNEL), vec(1), vec(1), vec(1), vec(SC_KERNEL)],
        out_specs=pl.BlockSpec((ts, 2 * cw), lambda b, i: (b * nt + i, 0)),
        out_shape=jax.ShapeDtypeStruct((nb * sl, 2 * cw), BF16),
        scratch_shapes=[pltpu.VMEM((ts + 2 * hl, cw), F32), pltpu.VMEM((ts + 2 * hl, cw), F32)],
        compiler_params=_cparams("arbitrary", "arbitrary"),
        name="conv_mixer",
    )(*([y_in] * 13), cfm_w, cfm_b.reshape(1, cw), ln_g.reshape(1, cw), ln_b.reshape(1, cw), sc_w)


_NO_RANK = 1 << 20


def _top_k_rows(s, rank=None, payload=None):
    if rank is None:
        rank = lax.broadcasted_iota(jnp.int32, s.shape, 0)
    vals, idxs = [], []
    for _ in range(PEER_TOPK):
        m = jnp.max(s, axis=0, keepdims=True)
        idx = jnp.min(jnp.where(s == m, rank, _NO_RANK), axis=0, keepdims=True)
        hit = rank == idx
        vals.append(m)
        idxs.append(idx if payload is None else jnp.sum(jnp.where(hit, payload, 0), axis=0, keepdims=True))
        s = jnp.where(hit, -jnp.inf, s)
    return jnp.concatenate(vals, axis=0), jnp.concatenate(idxs, axis=0)


def _pair_candidates(v1, i1, v2, i2):
    k = PEER_TOPK
    r = lax.broadcasted_iota(jnp.int32, (SUBLANES, v1.shape[1]), 0)
    lo, hi = slice(0, SUBLANES), slice(SUBLANES, 2 * SUBLANES)
    groups = []

    def row_a(a, b_rows, n_valid):
        b0 = b_rows.start
        groups.append((v1[a:a + 1] + v2[b_rows], i1[a:a + 1] * PEER_N_KEYS + i2[b_rows], a * k + b0 + r,
                       None if n_valid == SUBLANES else r < n_valid))

    def col_b(b, a_rows, valid):
        a0 = a_rows.start
        groups.append((v1[a_rows] + v2[b:b + 1], i1[a_rows] * PEER_N_KEYS + i2[b:b + 1], (a0 + r) * k + b, valid))

    row_a(0, lo, 8)
    row_a(0, hi, 8)
    row_a(1, lo, 8)
    col_b(0, hi, None)
    row_a(2, lo, k // 3)
    row_a(3, lo, k // 4)
    row_a(4, lo, k // 5)
    col_b(0, lo, r >= 5)
    col_b(1, lo, r >= 5)
    vals = jnp.concatenate([v if ok is None else jnp.where(ok, v, -jnp.inf) for v, _, _, ok in groups], axis=0)
    flat = jnp.concatenate([f if ok is None else jnp.where(ok, f, _NO_RANK) for _, _, f, ok in groups], axis=0)
    ids = jnp.concatenate([c if ok is None else jnp.where(ok, c, 0) for _, c, _, ok in groups], axis=0)
    return vals, flat, ids


def _route_kernel(q_ref, k1_ref, k2_ref, ids_ref, gate_ref):
    half = PEER_KEY_DIM // 2
    q = q_ref[...].astype(BF16)
    dn = (((1,), (1,)), ((), ()))
    s1 = lax.dot_general(k1_ref[...], q[:, :half], dn, preferred_element_type=F32)
    s2 = lax.dot_general(k2_ref[...], q[:, half:], dn, preferred_element_type=F32)
    v1, i1 = _top_k_rows(s1)
    v2, i2 = _top_k_rows(s2)
    cand, flat, cand_id = _pair_candidates(v1, i1, v2, i2)
    best, ids = _top_k_rows(cand, flat, cand_id)
    e = jnp.exp(best - jnp.max(best, axis=0, keepdims=True))
    gate_ref[...] = e / jnp.sum(e, axis=0, keepdims=True)
    ids_ref[...] = ids


def _peer_route(q, key1, key2):
    m = q.shape[0]
    tm = ROW_TILE
    out_spec = pl.BlockSpec((PEER_TOPK, tm), lambda i, h: (h, i))
    return pl.pallas_call(
        _route_kernel,
        grid=(m // tm, PEER_HEADS),
        in_specs=[pl.BlockSpec((tm, PEER_KEY_DIM), lambda i, h: (i, h)),
                  pl.BlockSpec(key1.shape, lambda i, h: (0, 0)),
                  pl.BlockSpec(key2.shape, lambda i, h: (0, 0))],
        out_specs=[out_spec, out_spec],
        out_shape=[jax.ShapeDtypeStruct((PEER_PICKS, m), jnp.int32),
                   jax.ShapeDtypeStruct((PEER_PICKS, m), F32)],
        compiler_params=_cparams("arbitrary", "arbitrary"),
        name="peer_route",
    )(q, key1, key2)


def _gather_rows(tt):
    return tt * PEER_PICKS


def _issue_rows(ids_ref, id0, tab_ref, buf_ref, row0, n_rows, sem):
    for r in range(n_rows):
        g, k = divmod(row0 + r, SUBLANES)
        pltpu.make_async_copy(tab_ref.at[ids_ref[id0 + r]], buf_ref.at[g, pl.ds(k, 1)], sem).start(priority=r % 2)


def _issue_rows_loop(ids_ref, id0, tab_ref, buf_ref, n_rows, sem):
    unroll = 2 * SUBLANES

    def body(g, carry):
        base = pl.multiple_of(g * unroll, unroll)
        blk = buf_ref.at[pl.ds(g * 2, 2)]
        for k in range(unroll):
            pltpu.make_async_copy(tab_ref.at[ids_ref[id0 + base + k]], blk.at[k // SUBLANES, pl.ds(k % SUBLANES, 1)],
                                  sem).start(priority=k % 2)
        return carry

    lax.fori_loop(0, n_rows // unroll, body, 0)


def _wait_rows(buf_ref, sem):
    pltpu.make_async_copy(buf_ref, buf_ref, sem).wait()


def _mix_token(buf_ref, row0, t, h_ref, gate_ref, x_ref, mg_ref, o_ref):
    d = h_ref.shape[1]
    w = buf_ref[pl.ds(row0 // SUBLANES, PEER_PICKS // SUBLANES)].reshape(PEER_PICKS, d)
    ut = lax.bitcast_convert_type(w & jnp.uint32(0xFFFF0000), F32)
    vt = lax.bitcast_convert_type(w << 16, F32)
    s = jnp.sum(ut * h_ref[pl.ds(t, 1), :], axis=-1, keepdims=True)
    a = jax.nn.gelu(s) * gate_ref[:, pl.ds(t, 1)]
    y = jnp.sum(vt * a, axis=0, keepdims=True)
    o_ref[pl.ds(t, 1), :] = x_ref[pl.ds(t, 1), :] + mg_ref[...] * y


GATHER_RING = 4


def _gather_kernel(ids_cur, ids_nxt, h_ref, gate_ref, x_ref, mg_ref, fg_ref, uv_ref, o_ref, *rest, final_norm):
    bufs, sem = rest[:GATHER_RING], rest[GATHER_RING]
    i = pl.program_id(0)
    n = pl.num_programs(0)
    ahead = GATHER_RING // 2
    tg = h_ref.shape[0] // GATHER_RING
    rows = _gather_rows(tg)
    refs = (h_ref, gate_ref, x_ref, mg_ref, o_ref)

    @pl.when(i == 0)
    def _():
        for p in range(ahead):
            _issue_rows_loop(ids_cur, p * rows, uv_ref, bufs[p], rows, sem.at[p])

    for p in range(GATHER_RING):
        q = (p + ahead) % GATHER_RING
        ids_ref = ids_cur if p + ahead < GATHER_RING else ids_nxt
        _wait_rows(bufs[p], sem.at[p])
        for t in range(tg):
            _issue_rows(ids_ref, q * rows + t * PEER_PICKS, uv_ref, bufs[q], t * PEER_PICKS, PEER_PICKS, sem.at[q])
            _mix_token(bufs[p], t * PEER_PICKS, p * tg + t, *refs)

    @pl.when(i == n - 1)
    def _():
        for p in range(ahead):
            _wait_rows(bufs[p], sem.at[p])

    if final_norm:
        o_ref[...] = _rms(o_ref[...], fg_ref[...])


def _pack_expert_tables(u, v):
    row = (u.shape[0], 1, u.shape[1])
    hi = lax.bitcast_convert_type(u.astype(BF16), jnp.uint16).astype(jnp.uint32).reshape(row)
    lo = lax.bitcast_convert_type(v.astype(BF16), jnp.uint16).astype(jnp.uint32).reshape(row)
    return (hi << 16) | lo


def _peer_mix(h2, ids_t, gate_t, x, mod_gate, row_fn, final_g, uv, final_norm):
    m, d = h2.shape
    tt = GATHER_TOKENS
    steps = m // tt
    rows = _gather_rows(tt)
    ids = ids_t.T.reshape(m * PEER_PICKS)
    gate = gate_t.reshape(PEER_PICKS, steps, tt).transpose(1, 0, 2)
    ids_spec = lambda f: pl.BlockSpec((rows,), f, memory_space=pltpu.SMEM)
    tok_spec = pl.BlockSpec((tt, d), lambda i: (i, 0))
    return pl.pallas_call(
        functools.partial(_gather_kernel, final_norm=final_norm),
        grid=(steps,),
        in_specs=[ids_spec(lambda i: (i,)),
                  ids_spec(lambda i: (jnp.minimum(i + 1, steps - 1),)),
                  tok_spec,
                  pl.BlockSpec((None, PEER_PICKS, tt), lambda i: (i, 0, 0)),
                  tok_spec,
                  pl.BlockSpec((None, 1, d), lambda i: (row_fn(i), 0, 0)),
                  pl.BlockSpec((1, d), lambda i: (0, 0)),
                  pl.BlockSpec(memory_space=pl.ANY)],
        out_specs=tok_spec,
        out_shape=jax.ShapeDtypeStruct((m, d), F32),
        scratch_shapes=[pltpu.VMEM((rows // (GATHER_RING * SUBLANES), SUBLANES, d), jnp.uint32)] * GATHER_RING
        + [pltpu.SemaphoreType.DMA((GATHER_RING,))],
        compiler_params=_cparams("arbitrary"),
        name="peer_gather_mix",
    )(ids, ids, h2, gate, x, mod_gate, final_g.reshape(1, d), uv)


def _rope_tables(n_rows, d_rot):
    row = jnp.broadcast_to(jnp.arange(n_rows, dtype=F32)[:, None], (n_rows, GRID_W)).reshape(-1)
    col = jnp.broadcast_to(jnp.arange(GRID_W, dtype=F32)[None, :], (n_rows, GRID_W)).reshape(-1)
    quarter = d_rot // 4
    inv_freq = ROPE_THETA ** (-jnp.arange(quarter, dtype=F32) / quarter)
    ar = row[:, None] * inv_freq
    ac = col[:, None] * inv_freq
    ang = jnp.concatenate([ar, ar, ac, ac], axis=-1)
    return jnp.cos(ang), jnp.sin(ang)


def _rotate_half_cols(w):
    r1, r2, c1, c2 = jnp.split(w, 4, axis=-1)
    return jnp.concatenate([-r2, r1, -c2, c1], axis=-1)


def _peer(x2, mod, tile_row_fn, gather_row_fn, norm_g, w_q, key1, key2, u, v, final_g, final_norm):
    d = x2.shape[1]
    q, h2 = _norm_mm(x2, norm_g, mod[3], mod[4], tile_row_fn, w_q.astype(BF16), tn=d, emit_h=True)
    ids_t, gate_t = _peer_route(q, key1.astype(BF16), key2.astype(BF16))
    return _peer_mix(h2, ids_t, gate_t, x2, mod[5], gather_row_fn, final_g, _pack_expert_tables(u, v), final_norm)


def kernel(x, c, ctx, c_ctx, ada_w, ada_b, norm1_g, norm2_g, attn_w_in, mla_q_norm_g, mla_w_uq, mla_kv_norm_g, mla_w_ukv, gqa_q_norm_g, gqa_k_norm_g, attn_w_out, conv_w_in, cfm_dw_w, cfm_dw_b, cfm_ln_g, cfm_ln_b, sc_dw_w, conv_w_out, peer_w_q, peer_key1, peer_key2, peer_u, peer_v, final_norm_g):
    nb, sl, d = x.shape
    tc = ctx.shape[1]
    depth = ada_w.shape[0]
    assert sl % ROW_TILE == 0 and tc % ROW_TILE == 0 and sl % GRID_W == 0 and d == 2048

    x2 = x.reshape(nb * sl, d)
    ctx2 = ctx.reshape(nb * tc, d)
    ctx_row = nb
    cc = jnp.zeros((SUBLANES, d), F32).at[:nb].set(c).at[ctx_row].set(c_ctx)
    lat_row = lambda i: (i * ROW_TILE) // sl
    ctx_row_fn = lambda i: ctx_row
    gat_row = lambda i: (i * GATHER_TOKENS) // sl

    cos_m, sin_m = _rope_tables(sl // GRID_W, MLA_ROPE_DIM)
    cos_g, sin_g = _rope_tables(sl // GRID_W, GQA_HEAD_DIM)
    one = lambda n, w: jnp.ones((n, w), F32)
    zero = lambda n, w: jnp.zeros((n, w), F32)

    for i in range(depth):
        mod = _modulation(cc, ada_w[i], ada_b[i]).reshape(SUBLANES, 6, 1, d).transpose(1, 0, 2, 3)
        if i % 2 == 0:
            a = i // 2
            w_in = attn_w_in[a]
            o_cq, o_ckv = MLA_Q_RANK, MLA_Q_RANK + MLA_KV_RANK
            o_kr = o_ckv + MLA_ROPE_DIM
            o_gk = o_kr + GQA_HEADS * GQA_HEAD_DIM
            o_gv = o_gk + GQA_KV_HEADS * GQA_HEAD_DIM
            w_kr = w_in[:, o_ckv:o_kr]
            w_in_r = jnp.concatenate([w_in[:, :o_ckv], w_in[:, o_kr:], w_kr, _rotate_half_cols(w_kr)], axis=1).astype(BF16)
            c_ckv = MLA_Q_RANK // MLA_KV_RANK
            c_gq = o_ckv // LANES
            c_gk = c_gq + GQA_HEADS
            c_gv = c_gk + GQA_KV_HEADS
            c_kr = c_gv + GQA_KV_HEADS
            wq = mla_w_uq[a].reshape(MLA_Q_RANK, MLA_HEADS, MLA_QK_DIM)
            wq = jnp.concatenate([wq, _rotate_half_cols(wq[..., MLA_NOPE_DIM:])], axis=-1).transpose(1, 0, 2).astype(BF16)
            wkv = mla_w_ukv[a].reshape(MLA_KV_RANK, MLA_HEADS, MLA_NOPE_DIM + MLA_V_DIM).transpose(1, 0, 2).astype(BF16)

            y_l = _norm_mm(x2, norm1_g[i], mod[0], mod[1], lat_row, w_in_r, tn=w_in_r.shape[1])
            y_c = _norm_mm(ctx2, norm1_g[i], mod[0], mod[1], ctx_row_fn, w_in_r, tn=w_in_r.shape[1])

            mq = _mla_q(y_l, mla_q_norm_g[a], wq, cos_m, sin_m, nb, sl)
            mk, mv = _mla_kv(y_l, c_ckv, c_kr, mla_kv_norm_g[a], wkv, cos_m, sin_m, nb, sl)
            mkc, mvc = _mla_kv(y_c, c_ckv, c_kr, mla_kv_norm_g[a], wkv, one(tc, MLA_ROPE_DIM), zero(tc, MLA_ROPE_DIM), nb, tc)
            hd = GQA_HEAD_DIM
            gq = _head_prep(y_l, c_gq, GQA_HEADS, gqa_q_norm_g[a], cos_g, sin_g, nb, sl, True, True)
            gk = _head_prep(y_l, c_gk, GQA_KV_HEADS, gqa_k_norm_g[a], cos_g, sin_g, nb, sl, True, True)
            gv = _head_prep(y_l, c_gv, GQA_KV_HEADS, gqa_k_norm_g[a], cos_g, sin_g, nb, sl, False, False)
            gkc = _head_prep(y_c, c_gk, GQA_KV_HEADS, gqa_k_norm_g[a], one(tc, hd), zero(tc, hd), nb, tc, True, False)
            gvc = _head_prep(y_c, c_gv, GQA_KV_HEADS, gqa_k_norm_g[a], one(tc, hd), zero(tc, hd), nb, tc, False, False)

            o_m = _attention(mq, mkc, mk, mvc, mv, MLA_QK_DIM ** -0.5)
            o_g = _attention(gq, gkc, gk, gvc, gv, GQA_HEAD_DIM ** -0.5)
            w_out = attn_w_out[a].astype(BF16)
            n_m = MLA_HEADS * MLA_V_DIM
            x2 = _mm_residual([o_m, o_g], [w_out[:n_m], w_out[n_m:]], x2, mod[2], lat_row)
        else:
            mi = i // 2
            y = _norm_mm(x2, norm1_g[i], mod[0], mod[1], lat_row, conv_w_in[mi].astype(BF16), tn=1280)
            yc = _conv_mixer(y, cfm_dw_w[mi], cfm_dw_b[mi], cfm_ln_g[mi], cfm_ln_b[mi], sc_dw_w[mi], nb, sl)
            x2 = _mm_residual([yc], [conv_w_out[mi].astype(BF16)], x2, mod[2], lat_row)
        x2 = _peer(x2, mod, lat_row, gat_row, norm2_g[i], peer_w_q[i], peer_key1[i], peer_key2[i], peer_u[i], peer_v[i],
                   final_norm_g, final_norm=(i == depth - 1))
    return x2.reshape(nb, sl, d)
```

```python
import functools

import jax
import jax.numpy as jnp
from jax import lax
from jax.experimental import pallas as pl
from jax.experimental.pallas import tpu as pltpu

F32 = jnp.float32
BF16 = jnp.bfloat16

GRID_W = 64
ROPE_THETA = 10000.0
NORM_EPS = 1e-6
MLA_HEADS = 8
MLA_Q_RANK = 512
MLA_KV_RANK = 256
MLA_NOPE_DIM = 128
MLA_ROPE_DIM = 64
MLA_V_DIM = 128
MLA_QK_DIM = MLA_NOPE_DIM + MLA_ROPE_DIM
GQA_HEADS = 8
GQA_KV_HEADS = 2
GQA_HEAD_DIM = 128
CFM_WIDTH = 1024
CFM_KERNEL = 31
SC_WIDTH = 1024
SC_KERNEL = 3
PEER_HEADS = 8
PEER_KEY_DIM = 256
PEER_N_KEYS = 128
PEER_TOPK = 16
PEER_PICKS = PEER_HEADS * PEER_TOPK

LANES = 128
SUBLANES = 8
VMEM_LIMIT = 48 * 1024 * 1024

ROW_TILE = 256
ATTN_Q_TILE = 256
CONV_TILE = 256
CONV_HALO = 16
GATHER_TOKENS = 8


def _cparams(*sem):
    return pltpu.CompilerParams(dimension_semantics=sem, vmem_limit_bytes=VMEM_LIMIT)


def _mod_kernel(c_ref, w_ref, b_ref, o_ref):
    c = c_ref[...]
    a = (c * jax.nn.sigmoid(c)).astype(BF16)
    o_ref[...] = jnp.dot(a, w_ref[...].astype(BF16), preferred_element_type=F32) + b_ref[...]


def _modulation(cc, w, b):
    r, d = cc.shape
    n = w.shape[1]
    tn = 1024
    return pl.pallas_call(
        _mod_kernel,
        grid=(n // tn,),
        in_specs=[pl.BlockSpec((r, d), lambda j: (0, 0)),
                  pl.BlockSpec((d, tn), lambda j: (0, j)),
                  pl.BlockSpec((1, tn), lambda j: (0, j))],
        out_specs=pl.BlockSpec((r, tn), lambda j: (0, j)),
        out_shape=jax.ShapeDtypeStruct((r, n), F32),
        compiler_params=_cparams("arbitrary"),
        name="adaln_modulation",
    )(cc, w, b.reshape(1, n))


def _norm_mm_kernel(x_ref, g_ref, sh_ref, sc_ref, w_ref, o_ref, *rest, emit_h):
    h_scr = rest[-1]

    @pl.when(pl.program_id(1) == 0)
    def _():
        x = x_ref[...]
        y = x * lax.rsqrt(jnp.mean(x * x, axis=-1, keepdims=True) + NORM_EPS) * g_ref[...]
        h = y * (1 + sc_ref[...]) + sh_ref[...]
        h_scr[...] = h.astype(BF16)
        if emit_h:
            rest[0][...] = h

    o_ref[...] = jnp.dot(h_scr[...], w_ref[...], preferred_element_type=F32)


def _norm_mm(x, g, shift, scale, row_fn, w, tn, emit_h=False):
    m, d = x.shape
    n = w.shape[1]
    tm = ROW_TILE
    mod_spec = pl.BlockSpec((None, 1, d), lambda i, j: (row_fn(i), 0, 0))
    out_shape = [jax.ShapeDtypeStruct((m, n), F32)]
    out_specs = [pl.BlockSpec((tm, tn), lambda i, j: (i, j))]
    if emit_h:
        out_shape.append(jax.ShapeDtypeStruct((m, d), F32))
        out_specs.append(pl.BlockSpec((tm, d), lambda i, j: (i, 0)))
    res = pl.pallas_call(
        functools.partial(_norm_mm_kernel, emit_h=emit_h),
        grid=(m // tm, n // tn),
        in_specs=[pl.BlockSpec((tm, d), lambda i, j: (i, 0)),
                  pl.BlockSpec((1, d), lambda i, j: (0, 0)),
                  mod_spec, mod_spec,
                  pl.BlockSpec((d, tn), lambda i, j: (0, j))],
        out_specs=out_specs,
        out_shape=out_shape,
        scratch_shapes=[pltpu.VMEM((tm, d), BF16)],
        compiler_params=_cparams("arbitrary", "arbitrary"),
        name="norm_modulate_matmul",
    )(x, g.reshape(1, d), shift, scale, w)
    return res if emit_h else res[0]


def _mm_res_kernel(*refs, n_a):
    a_refs, w_refs = refs[:n_a], refs[n_a:2 * n_a]
    x_ref, gate_ref, o_ref = refs[2 * n_a:]
    acc = jnp.dot(a_refs[0][...], w_refs[0][...], preferred_element_type=F32)
    for a_ref, w_ref in zip(a_refs[1:], w_refs[1:]):
        acc = acc + jnp.dot(a_ref[...], w_ref[...], preferred_element_type=F32)
    o_ref[...] = x_ref[...] + gate_ref[...] * acc


def _mm_residual(a_list, w_list, x, gate, row_fn):
    m, d = x.shape
    tm = ROW_TILE
    n_a = len(a_list)
    in_specs = [pl.BlockSpec((tm, a.shape[1]), lambda i: (i, 0)) for a in a_list]
    in_specs += [pl.BlockSpec(w.shape, lambda i: (0, 0)) for w in w_list]
    in_specs += [pl.BlockSpec((tm, d), lambda i: (i, 0)),
                 pl.BlockSpec((None, 1, d), lambda i: (row_fn(i), 0, 0))]
    return pl.pallas_call(
        functools.partial(_mm_res_kernel, n_a=n_a),
        grid=(m // tm,),
        in_specs=in_specs,
        out_specs=pl.BlockSpec((tm, d), lambda i: (i, 0)),
        out_shape=jax.ShapeDtypeStruct((m, d), F32),
        compiler_params=_cparams("arbitrary"),
        name="matmul_gated_residual",
    )(*a_list, *w_list, x, gate)


def _rms(x, g):
    return x * lax.rsqrt(jnp.mean(x * x, axis=-1, keepdims=True) + NORM_EPS) * g


def _mla_q_kernel(cq_ref, g_ref, w_ref, cos_ref, sin_ref, o_ref, xn_scr):
    @pl.when(pl.program_id(2) == 0)
    def _():
        xn_scr[...] = _rms(cq_ref[...], g_ref[...]).astype(BF16)

    y = jnp.dot(xn_scr[...], w_ref[...], preferred_element_type=F32)
    nope = y[:, :MLA_NOPE_DIM]
    r = y[:, MLA_NOPE_DIM:MLA_QK_DIM]
    rr = y[:, MLA_QK_DIM:]
    o_ref[...] = jnp.concatenate([nope, r * cos_ref[...] + rr * sin_ref[...]], axis=-1).astype(BF16)


def _mla_q(y_in, g, w, cos, sin, nb, sl):
    tm = ROW_TILE
    return pl.pallas_call(
        _mla_q_kernel,
        grid=(nb, sl // tm, MLA_HEADS),
        in_specs=[pl.BlockSpec((tm, MLA_Q_RANK), lambda b, i, h: (b * (sl // tm) + i, 0)),
                  pl.BlockSpec((1, MLA_Q_RANK), lambda b, i, h: (0, 0)),
                  pl.BlockSpec((None, MLA_Q_RANK, 2 * LANES), lambda b, i, h: (h, 0, 0)),
                  pl.BlockSpec((tm, MLA_ROPE_DIM), lambda b, i, h: (i, 0)),
                  pl.BlockSpec((tm, MLA_ROPE_DIM), lambda b, i, h: (i, 0))],
        out_specs=pl.BlockSpec((None, None, tm, MLA_QK_DIM), lambda b, i, h: (b, h, i, 0)),
        out_shape=jax.ShapeDtypeStruct((nb, MLA_HEADS, sl, MLA_QK_DIM), BF16),
        scratch_shapes=[pltpu.VMEM((tm, MLA_Q_RANK), BF16)],
        compiler_params=_cparams("arbitrary", "arbitrary", "arbitrary"),
        name="mla_query_prep",
    )(y_in, g.reshape(1, -1), w, cos, sin)


def _mla_kv_kernel(ckv_ref, kr_ref, g_ref, w_ref, cos_ref, sin_ref, k_ref, v_ref, xn_scr):
    @pl.when(pl.program_id(2) == 0)
    def _():
        xn_scr[...] = _rms(ckv_ref[...], g_ref[...]).astype(BF16)

    y = jnp.dot(xn_scr[...], w_ref[...], preferred_element_type=F32)
    krb = kr_ref[...]
    kr = krb[:, :MLA_ROPE_DIM] * cos_ref[...] + krb[:, MLA_ROPE_DIM:] * sin_ref[...]
    k_ref[...] = jnp.concatenate([y[:, :MLA_NOPE_DIM], kr], axis=-1).astype(BF16)
    v_ref[...] = y[:, MLA_NOPE_DIM:].astype(BF16)


def _mla_kv(y_in, ckv_col, kr_col, g, w, cos, sin, nb, sl):
    tm = ROW_TILE
    return pl.pallas_call(
        _mla_kv_kernel,
        grid=(nb, sl // tm, MLA_HEADS),
        in_specs=[pl.BlockSpec((tm, MLA_KV_RANK), lambda b, i, h: (b * (sl // tm) + i, ckv_col)),
                  pl.BlockSpec((tm, LANES), lambda b, i, h: (b * (sl // tm) + i, kr_col)),
                  pl.BlockSpec((1, MLA_KV_RANK), lambda b, i, h: (0, 0)),
                  pl.BlockSpec((None, MLA_KV_RANK, 2 * LANES), lambda b, i, h: (h, 0, 0)),
                  pl.BlockSpec((tm, MLA_ROPE_DIM), lambda b, i, h: (i, 0)),
                  pl.BlockSpec((tm, MLA_ROPE_DIM), lambda b, i, h: (i, 0))],
        out_specs=[pl.BlockSpec((None, None, tm, MLA_QK_DIM), lambda b, i, h: (b, h, i, 0)),
                   pl.BlockSpec((None, None, tm, MLA_V_DIM), lambda b, i, h: (b, h, i, 0))],
        out_shape=[jax.ShapeDtypeStruct((nb, MLA_HEADS, sl, MLA_QK_DIM), BF16),
                   jax.ShapeDtypeStruct((nb, MLA_HEADS, sl, MLA_V_DIM), BF16)],
        scratch_shapes=[pltpu.VMEM((tm, MLA_KV_RANK), BF16)],
        compiler_params=_cparams("arbitrary", "arbitrary", "arbitrary"),
        name="mla_key_value_prep",
    )(y_in, y_in, g.reshape(1, -1), w, cos, sin)


def _head_kernel(x_ref, g_ref, cos_ref, sin_ref, o_ref, *, norm, rope):
    x = x_ref[...]
    if norm:
        x = _rms(x, g_ref[...])
    if rope:
        q = GQA_HEAD_DIM // 4
        lane = lax.broadcasted_iota(jnp.int32, x.shape, 1)
        first = (lane % (2 * q)) < q
        rot = jnp.where(first, -pltpu.roll(x, GQA_HEAD_DIM - q, 1), pltpu.roll(x, q, 1))
        x = x * cos_ref[...] + rot * sin_ref[...]
    o_ref[...] = x.astype(BF16)


def _head_prep(y_in, col0, n_heads, g, cos, sin, nb, sl, norm, rope):
    tm = ROW_TILE
    return pl.pallas_call(
        functools.partial(_head_kernel, norm=norm, rope=rope),
        grid=(nb, sl // tm, n_heads),
        in_specs=[pl.BlockSpec((tm, GQA_HEAD_DIM), lambda b, i, h: (b * (sl // tm) + i, col0 + h)),
                  pl.BlockSpec((1, GQA_HEAD_DIM), lambda b, i, h: (0, 0)),
                  pl.BlockSpec((tm, GQA_HEAD_DIM), lambda b, i, h: (i, 0)),
                  pl.BlockSpec((tm, GQA_HEAD_DIM), lambda b, i, h: (i, 0))],
        out_specs=pl.BlockSpec((None, None, tm, GQA_HEAD_DIM), lambda b, i, h: (b, h, i, 0)),
        out_shape=jax.ShapeDtypeStruct((nb, n_heads, sl, GQA_HEAD_DIM), BF16),
        compiler_params=_cparams("arbitrary", "arbitrary", "arbitrary"),
        name="gqa_head_prep",
    )(y_in, g.reshape(1, -1), cos, sin)


def _attn_kernel(q_ref, kc_ref, kl_ref, vc_ref, vl_ref, o_ref, *, scale):
    q = q_ref[...]
    dn = (((1,), (1,)), ((), ()))
    sc = lax.dot_general(q, kc_ref[...], dn, preferred_element_type=F32) * scale
    sl = lax.dot_general(q, kl_ref[...], dn, preferred_element_type=F32) * scale
    m = jnp.maximum(jnp.max(sc, axis=-1, keepdims=True), jnp.max(sl, axis=-1, keepdims=True))
    pc = jnp.exp(sc - m)
    pl_ = jnp.exp(sl - m)
    denom = jnp.sum(pc, axis=-1, keepdims=True) + jnp.sum(pl_, axis=-1, keepdims=True)
    o = jnp.dot(pc.astype(BF16), vc_ref[...], preferred_element_type=F32)
    o = o + jnp.dot(pl_.astype(BF16), vl_ref[...], preferred_element_type=F32)
    o_ref[...] = (o / denom).astype(BF16)


def _attention(q, kc, kl, vc, vl, scale):
    nb, nh, sl, dk = q.shape
    hk, tc, tl, dv = kc.shape[1], kc.shape[2], kl.shape[2], vc.shape[3]
    grp = nh // hk
    tq = ATTN_Q_TILE
    kv_spec = lambda t, d: pl.BlockSpec((None, None, t, d), lambda b, h, i: (b, h // grp, 0, 0))
    return pl.pallas_call(
        functools.partial(_attn_kernel, scale=scale),
        grid=(nb, nh, sl // tq),
        in_specs=[pl.BlockSpec((None, None, tq, dk), lambda b, h, i: (b, h, i, 0)),
                  kv_spec(tc, dk), kv_spec(tl, dk), kv_spec(tc, dv), kv_spec(tl, dv)],
        out_specs=pl.BlockSpec((tq, dv), lambda b, h, i: (b * (sl // tq) + i, h)),
        out_shape=jax.ShapeDtypeStruct((nb * sl, nh * dv), BF16),
        compiler_params=_cparams("arbitrary", "arbitrary", "arbitrary"),
        name="softmax_attention",
    )(q, kc, kl, vc, vl)


def _conv_kernel(a_ref, ag_ref, b_ref, c_ref, u_ref,
                 ap_ref, agp_ref, cp_ref, up_ref, an_ref, agn_ref, cn_ref, un_ref,
                 cw_ref, cb_ref, lg_ref, lb_ref, sw_ref, o_ref, ext_scr, ext2_scr):
    i = pl.program_id(1)
    n = pl.num_programs(1)
    ts = a_ref.shape[0]
    hl = CONV_HALO
    has_prev = (i > 0).astype(F32)
    has_next = (i + 1 < n).astype(F32)
    glu = lambda a, g: a * jax.nn.sigmoid(g)
    ext_scr[pl.ds(0, hl), :] = glu(ap_ref[...], agp_ref[...]) * has_prev
    ext_scr[pl.ds(hl, ts), :] = glu(a_ref[...], ag_ref[...])
    ext_scr[pl.ds(hl + ts, hl), :] = glu(an_ref[...], agn_ref[...]) * has_next
    ext2_scr[pl.ds(0, hl), :] = cp_ref[...] * up_ref[...] * has_prev
    ext2_scr[pl.ds(hl, ts), :] = c_ref[...] * u_ref[...]
    ext2_scr[pl.ds(hl + ts, hl), :] = cn_ref[...] * un_ref[...] * has_next

    pad = (CFM_KERNEL - 1) // 2
    acc = cw_ref[pl.ds(0, 1), :] * ext_scr[pl.ds(hl - pad, ts), :]
    for k in range(1, CFM_KERNEL):
        acc = acc + cw_ref[pl.ds(k, 1), :] * ext_scr[pl.ds(hl - pad + k, ts), :]
    acc = acc + cb_ref[...]
    mu = jnp.mean(acc, axis=-1, keepdims=True)
    xc = acc - mu
    yn = xc * lax.rsqrt(jnp.mean(xc * xc, axis=-1, keepdims=True) + NORM_EPS) * lg_ref[...] + lb_ref[...]
    yc = yn * jax.nn.sigmoid(yn)

    pad2 = (SC_KERNEL - 1) // 2
    acc2 = sw_ref[pl.ds(0, 1), :] * ext2_scr[pl.ds(hl - pad2, ts), :]
    for k in range(1, SC_KERNEL):
        acc2 = acc2 + sw_ref[pl.ds(k, 1), :] * ext2_scr[pl.ds(hl - pad2 + k, ts), :]
    yd = b_ref[...] * acc2
    o_ref[...] = jnp.concatenate([yc, yd], axis=-1).astype(BF16)


def _conv_mixer(y_in, cfm_w, cfm_b, ln_g, ln_b, sc_w, nb, sl):
    ts, hl, cw = CONV_TILE, CONV_HALO, CFM_WIDTH
    nt = sl // ts
    rb = ts // hl
    last = nb * sl // hl - 1
    cur = lambda col: pl.BlockSpec((ts, cw), lambda b, i: (b * nt + i, col))
    prev = lambda col: pl.BlockSpec((hl, cw), lambda b, i: (jnp.maximum((b * nt + i) * rb - 1, 0), col))
    nxt = lambda col: pl.BlockSpec((hl, cw), lambda b, i: (jnp.minimum((b * nt + i + 1) * rb, last), col))
    vec = lambda r: pl.BlockSpec((r, cw), lambda b, i: (0, 0))
    return pl.pallas_call(
        _conv_kernel,
        grid=(nb, nt),
        in_specs=[cur(0), cur(1), cur(2), cur(3), cur(4),
                  prev(0), prev(1), prev(3), prev(4), nxt(0), nxt(1), nxt(3), nxt(4),
                  vec(CFM_KERNEL), vec(1), vec(1), vec(1), vec(SC_KERNEL)],
        out_specs=pl.BlockSpec((ts, 2 * cw), lambda b, i: (b * nt + i, 0)),
        out_shape=jax.ShapeDtypeStruct((nb * sl, 2 * cw), BF16),
        scratch_shapes=[pltpu.VMEM((ts + 2 * hl, cw), F32), pltpu.VMEM((ts + 2 * hl, cw), F32)],
        compiler_params=_cparams("arbitrary", "arbitrary"),
        name="conv_mixer",
    )(*([y_in] * 13), cfm_w, cfm_b.reshape(1, cw), ln_g.reshape(1, cw), ln_b.reshape(1, cw), sc_w)


_NO_RANK = 1 << 20


def _top_k_rows(s, rank=None, payload=None):
    if rank is None:
        rank = lax.broadcasted_iota(jnp.int32, s.shape, 0)
    vals, idxs = [], []
    for _ in range(PEER_TOPK):
        m = jnp.max(s, axis=0, keepdims=True)
        idx = jnp.min(jnp.where(s == m, rank, _NO_RANK), axis=0, keepdims=True)
        hit = rank == idx
        vals.append(m)
        idxs.append(idx if payload is None else jnp.sum(jnp.where(hit, payload, 0), axis=0, keepdims=True))
        s = jnp.where(hit, -jnp.inf, s)
    return jnp.concatenate(vals, axis=0), jnp.concatenate(idxs, axis=0)


def _pair_candidates(v1, i1, v2, i2):
    k = PEER_TOPK
    r = lax.broadcasted_iota(jnp.int32, (SUBLANES, v1.shape[1]), 0)
    lo, hi = slice(0, SUBLANES), slice(SUBLANES, 2 * SUBLANES)
    groups = []

    def row_a(a, b_rows, n_valid):
        b0 = b_rows.start
        groups.append((v1[a:a + 1] + v2[b_rows], i1[a:a + 1] * PEER_N_KEYS + i2[b_rows], a * k + b0 + r,
                       None if n_valid == SUBLANES else r < n_valid))

    def col_b(b, a_rows, valid):
        a0 = a_rows.start
        groups.append((v1[a_rows] + v2[b:b + 1], i1[a_rows] * PEER_N_KEYS + i2[b:b + 1], (a0 + r) * k + b, valid))

    row_a(0, lo, 8)
    row_a(0, hi, 8)
    row_a(1, lo, 8)
    col_b(0, hi, None)
    row_a(2, lo, k // 3)
    row_a(3, lo, k // 4)
    row_a(4, lo, k // 5)
    col_b(0, lo, r >= 5)
    col_b(1, lo, r >= 5)
    vals = jnp.concatenate([v if ok is None else jnp.where(ok, v, -jnp.inf) for v, _, _, ok in groups], axis=0)
    flat = jnp.concatenate([f if ok is None else jnp.where(ok, f, _NO_RANK) for _, _, f, ok in groups], axis=0)
    ids = jnp.concatenate([c if ok is None else jnp.where(ok, c, 0) for _, c, _, ok in groups], axis=0)
    return vals, flat, ids


def _route_kernel(q_ref, k1_ref, k2_ref, ids_ref, gate_ref):
    half = PEER_KEY_DIM // 2
    q = q_ref[...].astype(BF16)
    dn = (((1,), (1,)), ((), ()))
    s1 = lax.dot_general(k1_ref[...], q[:, :half], dn, preferred_element_type=F32)
    s2 = lax.dot_general(k2_ref[...], q[:, half:], dn, preferred_element_type=F32)
    v1, i1 = _top_k_rows(s1)
    v2, i2 = _top_k_rows(s2)
    cand, flat, cand_id = _pair_candidates(v1, i1, v2, i2)
    best, ids = _top_k_rows(cand, flat, cand_id)
    e = jnp.exp(best - jnp.max(best, axis=0, keepdims=True))
    gate_ref[...] = e / jnp.sum(e, axis=0, keepdims=True)
    ids_ref[...] = ids


def _peer_route(q, key1, key2):
    m = q.shape[0]
    tm = ROW_TILE
    out_spec = pl.BlockSpec((PEER_TOPK, tm), lambda i, h: (h, i))
    return pl.pallas_call(
        _route_kernel,
        grid=(m // tm, PEER_HEADS),
        in_specs=[pl.BlockSpec((tm, PEER_KEY_DIM), lambda i, h: (i, h)),
                  pl.BlockSpec(key1.shape, lambda i, h: (0, 0)),
                  pl.BlockSpec(key2.shape, lambda i, h: (0, 0))],
        out_specs=[out_spec, out_spec],
        out_shape=[jax.ShapeDtypeStruct((PEER_PICKS, m), jnp.int32),
                   jax.ShapeDtypeStruct((PEER_PICKS, m), F32)],
        compiler_params=_cparams("arbitrary", "arbitrary"),
        name="peer_route",
    )(q, key1, key2)


def _gather_rows(tt):
    return tt * PEER_PICKS


def _issue_rows(ids_ref, id0, tab_ref, buf_ref, row0, n_rows, sem):
    for r in range(n_rows):
        g, k = divmod(row0 + r, SUBLANES)
        pltpu.make_async_copy(tab_ref.at[ids_ref[id0 + r]], buf_ref.at[g, pl.ds(k, 1)], sem).start(priority=r % 2)


def _issue_rows_loop(ids_ref, id0, tab_ref, buf_ref, n_rows, sem):
    unroll = 2 * SUBLANES

    def body(g, carry):
        base = pl.multiple_of(g * unroll, unroll)
        blk = buf_ref.at[pl.ds(g * 2, 2)]
        for k in range(unroll):
            pltpu.make_async_copy(tab_ref.at[ids_ref[id0 + base + k]], blk.at[k // SUBLANES, pl.ds(k % SUBLANES, 1)],
                                  sem).start(priority=k % 2)
        return carry

    lax.fori_loop(0, n_rows // unroll, body, 0)


def _wait_rows(buf_ref, sem):
    pltpu.make_async_copy(buf_ref, buf_ref, sem).wait()


def _mix_token(buf_ref, row0, t, h_ref, gate_ref, x_ref, mg_ref, o_ref):
    d = h_ref.shape[1]
    w = buf_ref[pl.ds(row0 // SUBLANES, PEER_PICKS // SUBLANES)].reshape(PEER_PICKS, d)
    ut = lax.bitcast_convert_type(w & jnp.uint32(0xFFFF0000), F32)
    vt = lax.bitcast_convert_type(w << 16, F32)
    s = jnp.sum(ut * h_ref[pl.ds(t, 1), :], axis=-1, keepdims=True)
    a = jax.nn.gelu(s) * gate_ref[:, pl.ds(t, 1)]
    y = jnp.sum(vt * a, axis=0, keepdims=True)
    o_ref[pl.ds(t, 1), :] = x_ref[pl.ds(t, 1), :] + mg_ref[...] * y


GATHER_RING = 4


def _gather_kernel(ids_cur, ids_nxt, h_ref, gate_ref, x_ref, mg_ref, fg_ref, uv_ref, o_ref, *rest, final_norm):
    bufs, sem = rest[:GATHER_RING], rest[GATHER_RING]
    i = pl.program_id(0)
    n = pl.num_programs(0)
    ahead = GATHER_RING // 2
    tg = h_ref.shape[0] // GATHER_RING
    rows = _gather_rows(tg)
    refs = (h_ref, gate_ref, x_ref, mg_ref, o_ref)

    @pl.when(i == 0)
    def _():
        for p in range(ahead):
            _issue_rows_loop(ids_cur, p * rows, uv_ref, bufs[p], rows, sem.at[p])

    for p in range(GATHER_RING):
        q = (p + ahead) % GATHER_RING
        ids_ref = ids_cur if p + ahead < GATHER_RING else ids_nxt
        _wait_rows(bufs[p], sem.at[p])
        for t in range(tg):
            _issue_rows(ids_ref, q * rows + t * PEER_PICKS, uv_ref, bufs[q], t * PEER_PICKS, PEER_PICKS, sem.at[q])
            _mix_token(bufs[p], t * PEER_PICKS, p * tg + t, *refs)

    @pl.when(i == n - 1)
    def _():
        for p in range(ahead):
            _wait_rows(bufs[p], sem.at[p])

    if final_norm:
        o_ref[...] = _rms(o_ref[...], fg_ref[...])


PACK_ROWS = 256


def _pack_kernel(u_ref, v_ref, o_ref, scr, sem):
    i = pl.program_id(0)
    r, d = u_ref.shape
    bits = lambda x: lax.bitcast_convert_type(x.astype(BF16).astype(F32), jnp.uint32)
    scr[...] = (bits(u_ref[...]) | (bits(v_ref[...]) >> 16)).reshape(r // SUBLANES, SUBLANES, d)
    def body(g, carry):
        for k in range(SUBLANES):
            pltpu.make_async_copy(scr.at[g, pl.ds(k, 1)], o_ref.at[i * r + g * SUBLANES + k],
                                  sem.at[0]).start(priority=k % 2)
        return carry

    lax.fori_loop(0, r // SUBLANES, body, 0)
    pltpu.make_async_copy(scr, scr, sem.at[0]).wait()


def _pack_expert_tables(u, v, layer):
    _, e, d = u.shape
    r = PACK_ROWS
    blk = pl.BlockSpec((None, r, d), lambda i: (layer, i, 0))
    return pl.pallas_call(
        _pack_kernel,
        grid=(e // r,),
        in_specs=[blk, blk],
        out_specs=pl.BlockSpec(memory_space=pl.ANY),
        out_shape=jax.ShapeDtypeStruct((e, 1, d), jnp.uint32),
        scratch_shapes=[pltpu.VMEM((r // SUBLANES, SUBLANES, d), jnp.uint32), pltpu.SemaphoreType.DMA((1,))],
        compiler_params=_cparams("arbitrary"),
        name="pack_expert_rows",
    )(u, v)


def _peer_mix(h2, ids_t, gate_t, x, mod_gate, row_fn, final_g, uv, final_norm):
    m, d = h2.shape
    tt = GATHER_TOKENS
    steps = m // tt
    rows = _gather_rows(tt)
    ids = ids_t.T.reshape(m * PEER_PICKS)
    gate = gate_t.reshape(PEER_PICKS, steps, tt).transpose(1, 0, 2)
    ids_spec = lambda f: pl.BlockSpec((rows,), f, memory_space=pltpu.SMEM)
    tok_spec = pl.BlockSpec((tt, d), lambda i: (i, 0))
    return pl.pallas_call(
        functools.partial(_gather_kernel, final_norm=final_norm),
        grid=(steps,),
        in_specs=[ids_spec(lambda i: (i,)),
                  ids_spec(lambda i: (jnp.minimum(i + 1, steps - 1),)),
                  tok_spec,
                  pl.BlockSpec((None, PEER_PICKS, tt), lambda i: (i, 0, 0)),
                  tok_spec,
                  pl.BlockSpec((None, 1, d), lambda i: (row_fn(i), 0, 0)),
                  pl.BlockSpec((1, d), lambda i: (0, 0)),
                  pl.BlockSpec(memory_space=pl.ANY)],
        out_specs=tok_spec,
        out_shape=jax.ShapeDtypeStruct((m, d), F32),
        scratch_shapes=[pltpu.VMEM((rows // (GATHER_RING * SUBLANES), SUBLANES, d), jnp.uint32)] * GATHER_RING
        + [pltpu.SemaphoreType.DMA((GATHER_RING,))],
        compiler_params=_cparams("arbitrary"),
        name="peer_gather_mix",
    )(ids, ids, h2, gate, x, mod_gate, final_g.reshape(1, d), uv)


def _rope_tables(n_rows, d_rot):
    row = jnp.broadcast_to(jnp.arange(n_rows, dtype=F32)[:, None], (n_rows, GRID_W)).reshape(-1)
    col = jnp.broadcast_to(jnp.arange(GRID_W, dtype=F32)[None, :], (n_rows, GRID_W)).reshape(-1)
    quarter = d_rot // 4
    inv_freq = ROPE_THETA ** (-jnp.arange(quarter, dtype=F32) / quarter)
    ar = row[:, None] * inv_freq
    ac = col[:, None] * inv_freq
    ang = jnp.concatenate([ar, ar, ac, ac], axis=-1)
    return jnp.cos(ang), jnp.sin(ang)


def _rotate_half_cols(w):
    r1, r2, c1, c2 = jnp.split(w, 4, axis=-1)
    return jnp.concatenate([-r2, r1, -c2, c1], axis=-1)


def _peer(x2, mod, tile_row_fn, gather_row_fn, norm_g, w_q, key1, key2, uv, final_g, final_norm):
    d = x2.shape[1]
    q, h2 = _norm_mm(x2, norm_g, mod[3], mod[4], tile_row_fn, w_q.astype(BF16), tn=d, emit_h=True)
    ids_t, gate_t = _peer_route(q, key1.astype(BF16), key2.astype(BF16))
    return _peer_mix(h2, ids_t, gate_t, x2, mod[5], gather_row_fn, final_g, uv, final_norm)


def kernel(x, c, ctx, c_ctx, ada_w, ada_b, norm1_g, norm2_g, attn_w_in, mla_q_norm_g, mla_w_uq, mla_kv_norm_g, mla_w_ukv, gqa_q_norm_g, gqa_k_norm_g, attn_w_out, conv_w_in, cfm_dw_w, cfm_dw_b, cfm_ln_g, cfm_ln_b, sc_dw_w, conv_w_out, peer_w_q, peer_key1, peer_key2, peer_u, peer_v, final_norm_g):
    nb, sl, d = x.shape
    tc = ctx.shape[1]
    depth = ada_w.shape[0]
    assert sl % ROW_TILE == 0 and tc % ROW_TILE == 0 and sl % GRID_W == 0 and d == 2048

    x2 = x.reshape(nb * sl, d)
    ctx2 = ctx.reshape(nb * tc, d)
    ctx_row = nb
    cc = jnp.zeros((SUBLANES, d), F32).at[:nb].set(c).at[ctx_row].set(c_ctx)
    lat_row = lambda i: (i * ROW_TILE) // sl
    ctx_row_fn = lambda i: ctx_row
    gat_row = lambda i: (i * GATHER_TOKENS) // sl

    cos_m, sin_m = _rope_tables(sl // GRID_W, MLA_ROPE_DIM)
    cos_g, sin_g = _rope_tables(sl // GRID_W, GQA_HEAD_DIM)
    one = lambda n, w: jnp.ones((n, w), F32)
    zero = lambda n, w: jnp.zeros((n, w), F32)

    for i in range(depth):
        mod = _modulation(cc, ada_w[i], ada_b[i]).reshape(SUBLANES, 6, 1, d).transpose(1, 0, 2, 3)
        if i % 2 == 0:
            a = i // 2
            w_in = attn_w_in[a]
            o_cq, o_ckv = MLA_Q_RANK, MLA_Q_RANK + MLA_KV_RANK
            o_kr = o_ckv + MLA_ROPE_DIM
            o_gk = o_kr + GQA_HEADS * GQA_HEAD_DIM
            o_gv = o_gk + GQA_KV_HEADS * GQA_HEAD_DIM
            w_kr = w_in[:, o_ckv:o_kr]
            w_in_r = jnp.concatenate([w_in[:, :o_ckv], w_in[:, o_kr:], w_kr, _rotate_half_cols(w_kr)], axis=1).astype(BF16)
            c_ckv = MLA_Q_RANK // MLA_KV_RANK
            c_gq = o_ckv // LANES
            c_gk = c_gq + GQA_HEADS
            c_gv = c_gk + GQA_KV_HEADS
            c_kr = c_gv + GQA_KV_HEADS
            wq = mla_w_uq[a].reshape(MLA_Q_RANK, MLA_HEADS, MLA_QK_DIM)
            wq = jnp.concatenate([wq, _rotate_half_cols(wq[..., MLA_NOPE_DIM:])], axis=-1).transpose(1, 0, 2).astype(BF16)
            wkv = mla_w_ukv[a].reshape(MLA_KV_RANK, MLA_HEADS, MLA_NOPE_DIM + MLA_V_DIM).transpose(1, 0, 2).astype(BF16)

            y_l = _norm_mm(x2, norm1_g[i], mod[0], mod[1], lat_row, w_in_r, tn=w_in_r.shape[1])
            y_c = _norm_mm(ctx2, norm1_g[i], mod[0], mod[1], ctx_row_fn, w_in_r, tn=w_in_r.shape[1])

            mq = _mla_q(y_l, mla_q_norm_g[a], wq, cos_m, sin_m, nb, sl)
            mk, mv = _mla_kv(y_l, c_ckv, c_kr, mla_kv_norm_g[a], wkv, cos_m, sin_m, nb, sl)
            mkc, mvc = _mla_kv(y_c, c_ckv, c_kr, mla_kv_norm_g[a], wkv, one(tc, MLA_ROPE_DIM), zero(tc, MLA_ROPE_DIM), nb, tc)
            hd = GQA_HEAD_DIM
            gq = _head_prep(y_l, c_gq, GQA_HEADS, gqa_q_norm_g[a], cos_g, sin_g, nb, sl, True, True)
            gk = _head_prep(y_l, c_gk, GQA_KV_HEADS, gqa_k_norm_g[a], cos_g, sin_g, nb, sl, True, True)
            gv = _head_prep(y_l, c_gv, GQA_KV_HEADS, gqa_k_norm_g[a], cos_g, sin_g, nb, sl, False, False)
            gkc = _head_prep(y_c, c_gk, GQA_KV_HEADS, gqa_k_norm_g[a], one(tc, hd), zero(tc, hd), nb, tc, True, False)
            gvc = _head_prep(y_c, c_gv, GQA_KV_HEADS, gqa_k_norm_g[a], one(tc, hd), zero(tc, hd), nb, tc, False, False)

            o_m = _attention(mq, mkc, mk, mvc, mv, MLA_QK_DIM ** -0.5)
            o_g = _attention(gq, gkc, gk, gvc, gv, GQA_HEAD_DIM ** -0.5)
            w_out = attn_w_out[a].astype(BF16)
            n_m = MLA_HEADS * MLA_V_DIM
            x2 = _mm_residual([o_m, o_g], [w_out[:n_m], w_out[n_m:]], x2, mod[2], lat_row)
        else:
            mi = i // 2
            y = _norm_mm(x2, norm1_g[i], mod[0], mod[1], lat_row, conv_w_in[mi].astype(BF16), tn=1280)
            yc = _conv_mixer(y, cfm_dw_w[mi], cfm_dw_b[mi], cfm_ln_g[mi], cfm_ln_b[mi], sc_dw_w[mi], nb, sl)
            x2 = _mm_residual([yc], [conv_w_out[mi].astype(BF16)], x2, mod[2], lat_row)
        x2 = _peer(x2, mod, lat_row, gat_row, norm2_g[i], peer_w_q[i], peer_key1[i], peer_key2[i],
                   _pack_expert_tables(peer_u, peer_v, i), final_norm_g, final_norm=(i == depth - 1))
    return x2.reshape(nb, sl, d)
```

```python
import functools

import jax
import jax.numpy as jnp
from jax import lax
from jax.experimental import pallas as pl
from jax.experimental.pallas import tpu as pltpu

F32 = jnp.float32
BF16 = jnp.bfloat16

GRID_W = 64
ROPE_THETA = 10000.0
NORM_EPS = 1e-6
MLA_HEADS = 8
MLA_Q_RANK = 512
MLA_KV_RANK = 256
MLA_NOPE_DIM = 128
MLA_ROPE_DIM = 64
MLA_V_DIM = 128
MLA_QK_DIM = MLA_NOPE_DIM + MLA_ROPE_DIM
GQA_HEADS = 8
GQA_KV_HEADS = 2
GQA_HEAD_DIM = 128
CFM_WIDTH = 1024
CFM_KERNEL = 31
SC_WIDTH = 1024
SC_KERNEL = 3
PEER_HEADS = 8
PEER_KEY_DIM = 256
PEER_N_KEYS = 128
PEER_TOPK = 16
PEER_PICKS = PEER_HEADS * PEER_TOPK

LANES = 128
SUBLANES = 8
VMEM_LIMIT = 48 * 1024 * 1024

ROW_TILE = 256
ATTN_Q_TILE = 256
CONV_TILE = 256
CONV_HALO = 16
GATHER_TOKENS = 8


def _cparams(*sem):
    return pltpu.CompilerParams(dimension_semantics=sem, vmem_limit_bytes=VMEM_LIMIT)


def _mod_kernel(c_ref, w_ref, b_ref, o_ref):
    c = c_ref[...]
    a = (c * jax.nn.sigmoid(c)).astype(BF16)
    o_ref[...] = jnp.dot(a, w_ref[...].astype(BF16), preferred_element_type=F32) + b_ref[...]


def _modulation(cc, w, b):
    r, d = cc.shape
    n = w.shape[1]
    tn = 1024
    return pl.pallas_call(
        _mod_kernel,
        grid=(n // tn,),
        in_specs=[pl.BlockSpec((r, d), lambda j: (0, 0)),
                  pl.BlockSpec((d, tn), lambda j: (0, j)),
                  pl.BlockSpec((1, tn), lambda j: (0, j))],
        out_specs=pl.BlockSpec((r, tn), lambda j: (0, j)),
        out_shape=jax.ShapeDtypeStruct((r, n), F32),
        compiler_params=_cparams("arbitrary"),
        name="adaln_modulation",
    )(cc, w, b.reshape(1, n))


def _norm_mm_kernel(x_ref, g_ref, sh_ref, sc_ref, w_ref, o_ref, *rest, emit_h):
    h_scr = rest[-1]

    @pl.when(pl.program_id(1) == 0)
    def _():
        x = x_ref[...]
        y = x * lax.rsqrt(jnp.mean(x * x, axis=-1, keepdims=True) + NORM_EPS) * g_ref[...]
        h = y * (1 + sc_ref[...]) + sh_ref[...]
        h_scr[...] = h.astype(BF16)
        if emit_h:
            rest[0][...] = h

    o_ref[...] = jnp.dot(h_scr[...], w_ref[...], preferred_element_type=F32)


def _norm_mm(x, g, shift, scale, row_fn, w, tn, emit_h=False):
    m, d = x.shape
    n = w.shape[1]
    tm = ROW_TILE
    mod_spec = pl.BlockSpec((None, 1, d), lambda i, j: (row_fn(i), 0, 0))
    out_shape = [jax.ShapeDtypeStruct((m, n), F32)]
    out_specs = [pl.BlockSpec((tm, tn), lambda i, j: (i, j))]
    if emit_h:
        out_shape.append(jax.ShapeDtypeStruct((m, d), F32))
        out_specs.append(pl.BlockSpec((tm, d), lambda i, j: (i, 0)))
    res = pl.pallas_call(
        functools.partial(_norm_mm_kernel, emit_h=emit_h),
        grid=(m // tm, n // tn),
        in_specs=[pl.BlockSpec((tm, d), lambda i, j: (i, 0)),
                  pl.BlockSpec((1, d), lambda i, j: (0, 0)),
                  mod_spec, mod_spec,
                  pl.BlockSpec((d, tn), lambda i, j: (0, j))],
        out_specs=out_specs,
        out_shape=out_shape,
        scratch_shapes=[pltpu.VMEM((tm, d), BF16)],
        compiler_params=_cparams("arbitrary", "arbitrary"),
        name="norm_modulate_matmul",
    )(x, g.reshape(1, d), shift, scale, w)
    return res if emit_h else res[0]


def _mm_res_kernel(*refs, n_a):
    a_refs, w_refs = refs[:n_a], refs[n_a:2 * n_a]
    x_ref, gate_ref, o_ref = refs[2 * n_a:]
    acc = jnp.dot(a_refs[0][...], w_refs[0][...], preferred_element_type=F32)
    for a_ref, w_ref in zip(a_refs[1:], w_refs[1:]):
        acc = acc + jnp.dot(a_ref[...], w_ref[...], preferred_element_type=F32)
    o_ref[...] = x_ref[...] + gate_ref[...] * acc


def _mm_residual(a_list, w_list, x, gate, row_fn):
    m, d = x.shape
    tm = ROW_TILE
    n_a = len(a_list)
    in_specs = [pl.BlockSpec((tm, a.shape[1]), lambda i: (i, 0)) for a in a_list]
    in_specs += [pl.BlockSpec(w.shape, lambda i: (0, 0)) for w in w_list]
    in_specs += [pl.BlockSpec((tm, d), lambda i: (i, 0)),
                 pl.BlockSpec((None, 1, d), lambda i: (row_fn(i), 0, 0))]
    return pl.pallas_call(
        functools.partial(_mm_res_kernel, n_a=n_a),
        grid=(m // tm,),
        in_specs=in_specs,
        out_specs=pl.BlockSpec((tm, d), lambda i: (i, 0)),
        out_shape=jax.ShapeDtypeStruct((m, d), F32),
        compiler_params=_cparams("arbitrary"),
        name="matmul_gated_residual",
    )(*a_list, *w_list, x, gate)


def _rms(x, g):
    return x * lax.rsqrt(jnp.mean(x * x, axis=-1, keepdims=True) + NORM_EPS) * g


def _mla_q_kernel(cq_ref, g_ref, w_ref, cos_ref, sin_ref, o_ref):
    xn = _rms(cq_ref[...], g_ref[...]).astype(BF16)
    cos, sin = cos_ref[...], sin_ref[...]
    for h in range(MLA_HEADS):
        y = jnp.dot(xn, w_ref[h], preferred_element_type=F32)
        r = y[:, MLA_NOPE_DIM:MLA_QK_DIM]
        rr = y[:, MLA_QK_DIM:]
        o_ref[h] = jnp.concatenate([y[:, :MLA_NOPE_DIM], r * cos + rr * sin], axis=-1).astype(BF16)


def _mla_q(y_in, g, w, cos, sin, nb, sl):
    tm = ROW_TILE
    return pl.pallas_call(
        _mla_q_kernel,
        grid=(nb, sl // tm),
        in_specs=[pl.BlockSpec((tm, MLA_Q_RANK), lambda b, i: (b * (sl // tm) + i, 0)),
                  pl.BlockSpec((1, MLA_Q_RANK), lambda b, i: (0, 0)),
                  pl.BlockSpec((MLA_HEADS, MLA_Q_RANK, 2 * LANES), lambda b, i: (0, 0, 0)),
                  pl.BlockSpec((tm, MLA_ROPE_DIM), lambda b, i: (i, 0)),
                  pl.BlockSpec((tm, MLA_ROPE_DIM), lambda b, i: (i, 0))],
        out_specs=pl.BlockSpec((None, MLA_HEADS, tm, MLA_QK_DIM), lambda b, i: (b, 0, i, 0)),
        out_shape=jax.ShapeDtypeStruct((nb, MLA_HEADS, sl, MLA_QK_DIM), BF16),
        compiler_params=_cparams("arbitrary", "arbitrary"),
        name="mla_query_prep",
    )(y_in, g.reshape(1, -1), w, cos, sin)


def _mla_kv_kernel(ckv_ref, kr_ref, g_ref, w_ref, cos_ref, sin_ref, k_ref, v_ref):
    xn = _rms(ckv_ref[...], g_ref[...]).astype(BF16)
    krb = kr_ref[...]
    kr = krb[:, :MLA_ROPE_DIM] * cos_ref[...] + krb[:, MLA_ROPE_DIM:] * sin_ref[...]
    for h in range(MLA_HEADS):
        y = jnp.dot(xn, w_ref[h], preferred_element_type=F32)
        k_ref[h] = jnp.concatenate([y[:, :MLA_NOPE_DIM], kr], axis=-1).astype(BF16)
        v_ref[h] = y[:, MLA_NOPE_DIM:].astype(BF16)


def _mla_kv(y_in, ckv_col, kr_col, g, w, cos, sin, nb, sl):
    tm = ROW_TILE
    return pl.pallas_call(
        _mla_kv_kernel,
        grid=(nb, sl // tm),
        in_specs=[pl.BlockSpec((tm, MLA_KV_RANK), lambda b, i: (b * (sl // tm) + i, ckv_col)),
                  pl.BlockSpec((tm, LANES), lambda b, i: (b * (sl // tm) + i, kr_col)),
                  pl.BlockSpec((1, MLA_KV_RANK), lambda b, i: (0, 0)),
                  pl.BlockSpec((MLA_HEADS, MLA_KV_RANK, 2 * LANES), lambda b, i: (0, 0, 0)),
                  pl.BlockSpec((tm, MLA_ROPE_DIM), lambda b, i: (i, 0)),
                  pl.BlockSpec((tm, MLA_ROPE_DIM), lambda b, i: (i, 0))],
        out_specs=[pl.BlockSpec((None, MLA_HEADS, tm, MLA_QK_DIM), lambda b, i: (b, 0, i, 0)),
                   pl.BlockSpec((None, MLA_HEADS, tm, MLA_V_DIM), lambda b, i: (b, 0, i, 0))],
        out_shape=[jax.ShapeDtypeStruct((nb, MLA_HEADS, sl, MLA_QK_DIM), BF16),
                   jax.ShapeDtypeStruct((nb, MLA_HEADS, sl, MLA_V_DIM), BF16)],
        compiler_params=_cparams("arbitrary", "arbitrary"),
        name="mla_key_value_prep",
    )(y_in, y_in, g.reshape(1, -1), w, cos, sin)


def _head_kernel(x_ref, g_ref, cos_ref, sin_ref, o_ref, *, col0, n_heads, norm, rope):
    for h in range(n_heads):
        x = x_ref[:, (col0 + h) * GQA_HEAD_DIM:(col0 + h + 1) * GQA_HEAD_DIM]
        if norm:
            x = _rms(x, g_ref[...])
        if rope:
            q = GQA_HEAD_DIM // 4
            lane = lax.broadcasted_iota(jnp.int32, x.shape, 1)
            first = (lane % (2 * q)) < q
            rot = jnp.where(first, -pltpu.roll(x, GQA_HEAD_DIM - q, 1), pltpu.roll(x, q, 1))
            x = x * cos_ref[...] + rot * sin_ref[...]
        o_ref[h] = x.astype(BF16)


def _head_prep(y_in, col0, n_heads, g, cos, sin, nb, sl, norm, rope):
    tm = ROW_TILE
    return pl.pallas_call(
        functools.partial(_head_kernel, col0=col0, n_heads=n_heads, norm=norm, rope=rope),
        grid=(nb, sl // tm),
        in_specs=[pl.BlockSpec((tm, y_in.shape[1]), lambda b, i: (b * (sl // tm) + i, 0)),
                  pl.BlockSpec((1, GQA_HEAD_DIM), lambda b, i: (0, 0)),
                  pl.BlockSpec((tm, GQA_HEAD_DIM), lambda b, i: (i, 0)),
                  pl.BlockSpec((tm, GQA_HEAD_DIM), lambda b, i: (i, 0))],
        out_specs=pl.BlockSpec((None, n_heads, tm, GQA_HEAD_DIM), lambda b, i: (b, 0, i, 0)),
        out_shape=jax.ShapeDtypeStruct((nb, n_heads, sl, GQA_HEAD_DIM), BF16),
        compiler_params=_cparams("arbitrary", "arbitrary"),
        name="gqa_head_prep",
    )(y_in, g.reshape(1, -1), cos, sin)


def _attn_kernel(q_ref, kc_ref, kl_ref, vc_ref, vl_ref, o_ref, *, scale):
    q = q_ref[...]
    dn = (((1,), (1,)), ((), ()))
    sc = lax.dot_general(q, kc_ref[...], dn, preferred_element_type=F32) * scale
    sl = lax.dot_general(q, kl_ref[...], dn, preferred_element_type=F32) * scale
    m = jnp.maximum(jnp.max(sc, axis=-1, keepdims=True), jnp.max(sl, axis=-1, keepdims=True))
    pc = jnp.exp(sc - m)
    pl_ = jnp.exp(sl - m)
    denom = jnp.sum(pc, axis=-1, keepdims=True) + jnp.sum(pl_, axis=-1, keepdims=True)
    o = jnp.dot(pc.astype(BF16), vc_ref[...], preferred_element_type=F32)
    o = o + jnp.dot(pl_.astype(BF16), vl_ref[...], preferred_element_type=F32)
    o_ref[...] = (o / denom).astype(BF16)


def _attention(q, kc, kl, vc, vl, scale):
    nb, nh, sl, dk = q.shape
    hk, tc, tl, dv = kc.shape[1], kc.shape[2], kl.shape[2], vc.shape[3]
    grp = nh // hk
    tq = ATTN_Q_TILE
    kv_spec = lambda t, d: pl.BlockSpec((None, None, t, d), lambda b, h, i: (b, h // grp, 0, 0))
    return pl.pallas_call(
        functools.partial(_attn_kernel, scale=scale),
        grid=(nb, nh, sl // tq),
        in_specs=[pl.BlockSpec((None, None, tq, dk), lambda b, h, i: (b, h, i, 0)),
                  kv_spec(tc, dk), kv_spec(tl, dk), kv_spec(tc, dv), kv_spec(tl, dv)],
        out_specs=pl.BlockSpec((tq, dv), lambda b, h, i: (b * (sl // tq) + i, h)),
        out_shape=jax.ShapeDtypeStruct((nb * sl, nh * dv), BF16),
        compiler_params=_cparams("arbitrary", "arbitrary", "arbitrary"),
        name="softmax_attention",
    )(q, kc, kl, vc, vl)


def _conv_kernel(a_ref, ag_ref, b_ref, c_ref, u_ref,
                 ap_ref, agp_ref, cp_ref, up_ref, an_ref, agn_ref, cn_ref, un_ref,
                 cw_ref, cb_ref, lg_ref, lb_ref, sw_ref, o_ref, ext_scr, ext2_scr):
    i = pl.program_id(1)
    n = pl.num_programs(1)
    ts = a_ref.shape[0]
    hl = CONV_HALO
    has_prev = (i > 0).astype(F32)
    has_next = (i + 1 < n).astype(F32)
    glu = lambda a, g: a * jax.nn.sigmoid(g)
    ext_scr[pl.ds(0, hl), :] = glu(ap_ref[...], agp_ref[...]) * has_prev
    ext_scr[pl.ds(hl, ts), :] = glu(a_ref[...], ag_ref[...])
    ext_scr[pl.ds(hl + ts, hl), :] = glu(an_ref[...], agn_ref[...]) * has_next
    ext2_scr[pl.ds(0, hl), :] = cp_ref[...] * up_ref[...] * has_prev
    ext2_scr[pl.ds(hl, ts), :] = c_ref[...] * u_ref[...]
    ext2_scr[pl.ds(hl + ts, hl), :] = cn_ref[...] * un_ref[...] * has_next

    pad = (CFM_KERNEL - 1) // 2
    acc = cw_ref[pl.ds(0, 1), :] * ext_scr[pl.ds(hl - pad, ts), :]
    for k in range(1, CFM_KERNEL):
        acc = acc + cw_ref[pl.ds(k, 1), :] * ext_scr[pl.ds(hl - pad + k, ts), :]
    acc = acc + cb_ref[...]
    mu = jnp.mean(acc, axis=-1, keepdims=True)
    xc = acc - mu
    yn = xc * lax.rsqrt(jnp.mean(xc * xc, axis=-1, keepdims=True) + NORM_EPS) * lg_ref[...] + lb_ref[...]
    yc = yn * jax.nn.sigmoid(yn)

    pad2 = (SC_KERNEL - 1) // 2
    acc2 = sw_ref[pl.ds(0, 1), :] * ext2_scr[pl.ds(hl - pad2, ts), :]
    for k in range(1, SC_KERNEL):
        acc2 = acc2 + sw_ref[pl.ds(k, 1), :] * ext2_scr[pl.ds(hl - pad2 + k, ts), :]
    yd = b_ref[...] * acc2
    o_ref[...] = jnp.concatenate([yc, yd], axis=-1).astype(BF16)


def _conv_mixer(y_in, cfm_w, cfm_b, ln_g, ln_b, sc_w, nb, sl):
    ts, hl, cw = CONV_TILE, CONV_HALO, CFM_WIDTH
    nt = sl // ts
    rb = ts // hl
    last = nb * sl // hl - 1
    cur = lambda col: pl.BlockSpec((ts, cw), lambda b, i: (b * nt + i, col))
    prev = lambda col: pl.BlockSpec((hl, cw), lambda b, i: (jnp.maximum((b * nt + i) * rb - 1, 0), col))
    nxt = lambda col: pl.BlockSpec((hl, cw), lambda b, i: (jnp.minimum((b * nt + i + 1) * rb, last), col))
    vec = lambda r: pl.BlockSpec((r, cw), lambda b, i: (0, 0))
    return pl.pallas_call(
        _conv_kernel,
        grid=(nb, nt),
        in_specs=[cur(0), cur(1), cur(2), cur(3), cur(4),
                  prev(0), prev(1), prev(3), prev(4), nxt(0), nxt(1), nxt(3), nxt(4),
                  vec(CFM_KERNEL), vec(1), vec(1), vec(1), vec(SC_KERNEL)],
        out_specs=pl.BlockSpec((ts, 2 * cw), lambda b, i: (b * nt + i, 0)),
        out_shape=jax.ShapeDtypeStruct((nb * sl, 2 * cw), BF16),
        scratch_shapes=[pltpu.VMEM((ts + 2 * hl, cw), F32), pltpu.VMEM((ts + 2 * hl, cw), F32)],
        compiler_params=_cparams("arbitrary", "arbitrary"),
        name="conv_mixer",
    )(*([y_in] * 13), cfm_w, cfm_b.reshape(1, cw), ln_g.reshape(1, cw), ln_b.reshape(1, cw), sc_w)


_NO_RANK = 1 << 20


def _top_k_rows(s, rank=None, payload=None):
    if rank is None:
        rank = lax.broadcasted_iota(jnp.int32, s.shape, 0)
    vals, idxs = [], []
    for _ in range(PEER_TOPK):
        m = jnp.max(s, axis=0, keepdims=True)
        idx = jnp.min(jnp.where(s == m, rank, _NO_RANK), axis=0, keepdims=True)
        hit = rank == idx
        vals.append(m)
        idxs.append(idx if payload is None else jnp.sum(jnp.where(hit, payload, 0), axis=0, keepdims=True))
        s = jnp.where(hit, -jnp.inf, s)
    return jnp.concatenate(vals, axis=0), jnp.concatenate(idxs, axis=0)


def _pair_candidates(v1, i1, v2, i2):
    k = PEER_TOPK
    r = lax.broadcasted_iota(jnp.int32, (SUBLANES, v1.shape[1]), 0)
    lo, hi = slice(0, SUBLANES), slice(SUBLANES, 2 * SUBLANES)
    groups = []

    def row_a(a, b_rows, n_valid):
        b0 = b_rows.start
        groups.append((v1[a:a + 1] + v2[b_rows], i1[a:a + 1] * PEER_N_KEYS + i2[b_rows], a * k + b0 + r,
                       None if n_valid == SUBLANES else r < n_valid))

    def col_b(b, a_rows, valid):
        a0 = a_rows.start
        groups.append((v1[a_rows] + v2[b:b + 1], i1[a_rows] * PEER_N_KEYS + i2[b:b + 1], (a0 + r) * k + b, valid))

    row_a(0, lo, 8)
    row_a(0, hi, 8)
    row_a(1, lo, 8)
    col_b(0, hi, None)
    row_a(2, lo, k // 3)
    row_a(3, lo, k // 4)
    row_a(4, lo, k // 5)
    col_b(0, lo, r >= 5)
    col_b(1, lo, r >= 5)
    vals = jnp.concatenate([v if ok is None else jnp.where(ok, v, -jnp.inf) for v, _, _, ok in groups], axis=0)
    flat = jnp.concatenate([f if ok is None else jnp.where(ok, f, _NO_RANK) for _, _, f, ok in groups], axis=0)
    ids = jnp.concatenate([c if ok is None else jnp.where(ok, c, 0) for _, c, _, ok in groups], axis=0)
    return vals, flat, ids


def _route_kernel(q_ref, k1_ref, k2_ref, ids_ref, gate_ref):
    half = PEER_KEY_DIM // 2
    q = q_ref[...].astype(BF16)
    dn = (((1,), (1,)), ((), ()))
    s1 = lax.dot_general(k1_ref[...], q[:, :half], dn, preferred_element_type=F32)
    s2 = lax.dot_general(k2_ref[...], q[:, half:], dn, preferred_element_type=F32)
    v1, i1 = _top_k_rows(s1)
    v2, i2 = _top_k_rows(s2)
    cand, flat, cand_id = _pair_candidates(v1, i1, v2, i2)
    best, ids = _top_k_rows(cand, flat, cand_id)
    e = jnp.exp(best - jnp.max(best, axis=0, keepdims=True))
    gate_ref[...] = e / jnp.sum(e, axis=0, keepdims=True)
    ids_ref[...] = ids


def _peer_route(q, key1, key2):
    m = q.shape[0]
    tm = ROW_TILE
    out_spec = pl.BlockSpec((PEER_TOPK, tm), lambda i, h: (h, i))
    return pl.pallas_call(
        _route_kernel,
        grid=(m // tm, PEER_HEADS),
        in_specs=[pl.BlockSpec((tm, PEER_KEY_DIM), lambda i, h: (i, h)),
                  pl.BlockSpec(key1.shape, lambda i, h: (0, 0)),
                  pl.BlockSpec(key2.shape, lambda i, h: (0, 0))],
        out_specs=[out_spec, out_spec],
        out_shape=[jax.ShapeDtypeStruct((PEER_PICKS, m), jnp.int32),
                   jax.ShapeDtypeStruct((PEER_PICKS, m), F32)],
        compiler_params=_cparams("arbitrary", "arbitrary"),
        name="peer_route",
    )(q, key1, key2)


def _gather_rows(tt):
    return tt * PEER_PICKS


def _issue_rows(ids_ref, id0, tab_ref, buf_ref, row0, n_rows, sem):
    for r in range(n_rows):
        g, k = divmod(row0 + r, SUBLANES)
        pltpu.make_async_copy(tab_ref.at[ids_ref[id0 + r]], buf_ref.at[g, pl.ds(k, 1)], sem).start(priority=r % 2)


def _issue_rows_loop(ids_ref, id0, tab_ref, buf_ref, n_rows, sem):
    unroll = 2 * SUBLANES

    def body(g, carry):
        base = pl.multiple_of(g * unroll, unroll)
        blk = buf_ref.at[pl.ds(g * 2, 2)]
        for k in range(unroll):
            pltpu.make_async_copy(tab_ref.at[ids_ref[id0 + base + k]], blk.at[k // SUBLANES, pl.ds(k % SUBLANES, 1)],
                                  sem).start(priority=k % 2)
        return carry

    lax.fori_loop(0, n_rows // unroll, body, 0)


def _wait_rows(buf_ref, sem):
    pltpu.make_async_copy(buf_ref, buf_ref, sem).wait()


def _mix_token(buf_ref, row0, t, h_ref, gate_ref, x_ref, mg_ref, o_ref):
    d = h_ref.shape[1]
    w = buf_ref[pl.ds(row0 // SUBLANES, PEER_PICKS // SUBLANES)].reshape(PEER_PICKS, d)
    ut = lax.bitcast_convert_type(w & jnp.uint32(0xFFFF0000), F32)
    vt = lax.bitcast_convert_type(w << 16, F32)
    s = jnp.sum(ut * h_ref[pl.ds(t, 1), :], axis=-1, keepdims=True)
    a = jax.nn.gelu(s) * gate_ref[:, pl.ds(t, 1)]
    y = jnp.sum(vt * a, axis=0, keepdims=True)
    o_ref[pl.ds(t, 1), :] = x_ref[pl.ds(t, 1), :] + mg_ref[...] * y


GATHER_RING = 4


def _gather_kernel(ids_cur, ids_nxt, h_ref, gate_ref, x_ref, mg_ref, fg_ref, uv_ref, o_ref, *rest, final_norm):
    bufs, sem = rest[:GATHER_RING], rest[GATHER_RING]
    i = pl.program_id(0)
    n = pl.num_programs(0)
    ahead = GATHER_RING // 2
    tg = h_ref.shape[0] // GATHER_RING
    rows = _gather_rows(tg)
    refs = (h_ref, gate_ref, x_ref, mg_ref, o_ref)

    @pl.when(i == 0)
    def _():
        for p in range(ahead):
            _issue_rows_loop(ids_cur, p * rows, uv_ref, bufs[p], rows, sem.at[p])

    for p in range(GATHER_RING):
        q = (p + ahead) % GATHER_RING
        ids_ref = ids_cur if p + ahead < GATHER_RING else ids_nxt
        _wait_rows(bufs[p], sem.at[p])
        for t in range(tg):
            _issue_rows(ids_ref, q * rows + t * PEER_PICKS, uv_ref, bufs[q], t * PEER_PICKS, PEER_PICKS, sem.at[q])
            _mix_token(bufs[p], t * PEER_PICKS, p * tg + t, *refs)

    @pl.when(i == n - 1)
    def _():
        for p in range(ahead):
            _wait_rows(bufs[p], sem.at[p])

    if final_norm:
        o_ref[...] = _rms(o_ref[...], fg_ref[...])


PACK_ROWS = 256


def _pack_kernel(u_ref, v_ref, o_ref, scr, sem):
    i = pl.program_id(0)
    r, d = u_ref.shape
    bits = lambda x: lax.bitcast_convert_type(x.astype(BF16).astype(F32), jnp.uint32)
    scr[...] = (bits(u_ref[...]) | (bits(v_ref[...]) >> 16)).reshape(r // SUBLANES, SUBLANES, d)
    def body(g, carry):
        for k in range(SUBLANES):
            pltpu.make_async_copy(scr.at[g, pl.ds(k, 1)], o_ref.at[i * r + g * SUBLANES + k],
                                  sem.at[0]).start(priority=k % 2)
        return carry

    lax.fori_loop(0, r // SUBLANES, body, 0)
    pltpu.make_async_copy(scr, scr, sem.at[0]).wait()


def _pack_expert_tables(u, v, layer):
    _, e, d = u.shape
    r = PACK_ROWS
    blk = pl.BlockSpec((None, r, d), lambda i: (layer, i, 0))
    return pl.pallas_call(
        _pack_kernel,
        grid=(e // r,),
        in_specs=[blk, blk],
        out_specs=pl.BlockSpec(memory_space=pl.ANY),
        out_shape=jax.ShapeDtypeStruct((e, 1, d), jnp.uint32),
        scratch_shapes=[pltpu.VMEM((r // SUBLANES, SUBLANES, d), jnp.uint32), pltpu.SemaphoreType.DMA((1,))],
        compiler_params=_cparams("arbitrary"),
        name="pack_expert_rows",
    )(u, v)


def _peer_mix(h2, ids_t, gate_t, x, mod_gate, row_fn, final_g, uv, final_norm):
    m, d = h2.shape
    tt = GATHER_TOKENS
    steps = m // tt
    rows = _gather_rows(tt)
    ids = ids_t.T.reshape(m * PEER_PICKS)
    gate = gate_t.reshape(PEER_PICKS, steps, tt).transpose(1, 0, 2)
    ids_spec = lambda f: pl.BlockSpec((rows,), f, memory_space=pltpu.SMEM)
    tok_spec = pl.BlockSpec((tt, d), lambda i: (i, 0))
    return pl.pallas_call(
        functools.partial(_gather_kernel, final_norm=final_norm),
        grid=(steps,),
        in_specs=[ids_spec(lambda i: (i,)),
                  ids_spec(lambda i: (jnp.minimum(i + 1, steps - 1),)),
                  tok_spec,
                  pl.BlockSpec((None, PEER_PICKS, tt), lambda i: (i, 0, 0)),
                  tok_spec,
                  pl.BlockSpec((None, 1, d), lambda i: (row_fn(i), 0, 0)),
                  pl.BlockSpec((1, d), lambda i: (0, 0)),
                  pl.BlockSpec(memory_space=pl.ANY)],
        out_specs=tok_spec,
        out_shape=jax.ShapeDtypeStruct((m, d), F32),
        scratch_shapes=[pltpu.VMEM((rows // (GATHER_RING * SUBLANES), SUBLANES, d), jnp.uint32)] * GATHER_RING
        + [pltpu.SemaphoreType.DMA((GATHER_RING,))],
        compiler_params=_cparams("arbitrary"),
        name="peer_gather_mix",
    )(ids, ids, h2, gate, x, mod_gate, final_g.reshape(1, d), uv)


def _rope_tables(n_rows, d_rot):
    row = jnp.broadcast_to(jnp.arange(n_rows, dtype=F32)[:, None], (n_rows, GRID_W)).reshape(-1)
    col = jnp.broadcast_to(jnp.arange(GRID_W, dtype=F32)[None, :], (n_rows, GRID_W)).reshape(-1)
    quarter = d_rot // 4
    inv_freq = ROPE_THETA ** (-jnp.arange(quarter, dtype=F32) / quarter)
    ar = row[:, None] * inv_freq
    ac = col[:, None] * inv_freq
    ang = jnp.concatenate([ar, ar, ac, ac], axis=-1)
    return jnp.cos(ang), jnp.sin(ang)


def _rotate_half_cols(w):
    r1, r2, c1, c2 = jnp.split(w, 4, axis=-1)
    return jnp.concatenate([-r2, r1, -c2, c1], axis=-1)


def _peer(x2, mod, tile_row_fn, gather_row_fn, norm_g, w_q, key1, key2, uv, final_g, final_norm):
    d = x2.shape[1]
    q, h2 = _norm_mm(x2, norm_g, mod[3], mod[4], tile_row_fn, w_q.astype(BF16), tn=d, emit_h=True)
    ids_t, gate_t = _peer_route(q, key1.astype(BF16), key2.astype(BF16))
    return _peer_mix(h2, ids_t, gate_t, x2, mod[5], gather_row_fn, final_g, uv, final_norm)


def kernel(x, c, ctx, c_ctx, ada_w, ada_b, norm1_g, norm2_g, attn_w_in, mla_q_norm_g, mla_w_uq, mla_kv_norm_g, mla_w_ukv, gqa_q_norm_g, gqa_k_norm_g, attn_w_out, conv_w_in, cfm_dw_w, cfm_dw_b, cfm_ln_g, cfm_ln_b, sc_dw_w, conv_w_out, peer_w_q, peer_key1, peer_key2, peer_u, peer_v, final_norm_g):
    nb, sl, d = x.shape
    tc = ctx.shape[1]
    depth = ada_w.shape[0]
    assert sl % ROW_TILE == 0 and tc % ROW_TILE == 0 and sl % GRID_W == 0 and d == 2048

    x2 = x.reshape(nb * sl, d)
    ctx2 = ctx.reshape(nb * tc, d)
    ctx_row = nb
    cc = jnp.zeros((SUBLANES, d), F32).at[:nb].set(c).at[ctx_row].set(c_ctx)
    lat_row = lambda i: (i * ROW_TILE) // sl
    ctx_row_fn = lambda i: ctx_row
    gat_row = lambda i: (i * GATHER_TOKENS) // sl

    cos_m, sin_m = _rope_tables(sl // GRID_W, MLA_ROPE_DIM)
    cos_g, sin_g = _rope_tables(sl // GRID_W, GQA_HEAD_DIM)
    one = lambda n, w: jnp.ones((n, w), F32)
    zero = lambda n, w: jnp.zeros((n, w), F32)

    for i in range(depth):
        mod = _modulation(cc, ada_w[i], ada_b[i]).reshape(SUBLANES, 6, 1, d).transpose(1, 0, 2, 3)
        if i % 2 == 0:
            a = i // 2
            w_in = attn_w_in[a]
            o_cq, o_ckv = MLA_Q_RANK, MLA_Q_RANK + MLA_KV_RANK
            o_kr = o_ckv + MLA_ROPE_DIM
            o_gk = o_kr + GQA_HEADS * GQA_HEAD_DIM
            o_gv = o_gk + GQA_KV_HEADS * GQA_HEAD_DIM
            w_kr = w_in[:, o_ckv:o_kr]
            w_in_r = jnp.concatenate([w_in[:, :o_ckv], w_in[:, o_kr:], w_kr, _rotate_half_cols(w_kr)], axis=1).astype(BF16)
            c_ckv = MLA_Q_RANK // MLA_KV_RANK
            c_gq = o_ckv // LANES
            c_gk = c_gq + GQA_HEADS
            c_gv = c_gk + GQA_KV_HEADS
            c_kr = c_gv + GQA_KV_HEADS
            wq = mla_w_uq[a].reshape(MLA_Q_RANK, MLA_HEADS, MLA_QK_DIM)
            wq = jnp.concatenate([wq, _rotate_half_cols(wq[..., MLA_NOPE_DIM:])], axis=-1).transpose(1, 0, 2).astype(BF16)
            wkv = mla_w_ukv[a].reshape(MLA_KV_RANK, MLA_HEADS, MLA_NOPE_DIM + MLA_V_DIM).transpose(1, 0, 2).astype(BF16)

            y_l = _norm_mm(x2, norm1_g[i], mod[0], mod[1], lat_row, w_in_r, tn=w_in_r.shape[1])
            y_c = _norm_mm(ctx2, norm1_g[i], mod[0], mod[1], ctx_row_fn, w_in_r, tn=w_in_r.shape[1])

            mq = _mla_q(y_l, mla_q_norm_g[a], wq, cos_m, sin_m, nb, sl)
            mk, mv = _mla_kv(y_l, c_ckv, c_kr, mla_kv_norm_g[a], wkv, cos_m, sin_m, nb, sl)
            mkc, mvc = _mla_kv(y_c, c_ckv, c_kr, mla_kv_norm_g[a], wkv, one(tc, MLA_ROPE_DIM), zero(tc, MLA_ROPE_DIM), nb, tc)
            hd = GQA_HEAD_DIM
            gq = _head_prep(y_l, c_gq, GQA_HEADS, gqa_q_norm_g[a], cos_g, sin_g, nb, sl, True, True)
            gk = _head_prep(y_l, c_gk, GQA_KV_HEADS, gqa_k_norm_g[a], cos_g, sin_g, nb, sl, True, True)
            gv = _head_prep(y_l, c_gv, GQA_KV_HEADS, gqa_k_norm_g[a], cos_g, sin_g, nb, sl, False, False)
            gkc = _head_prep(y_c, c_gk, GQA_KV_HEADS, gqa_k_norm_g[a], one(tc, hd), zero(tc, hd), nb, tc, True, False)
            gvc = _head_prep(y_c, c_gv, GQA_KV_HEADS, gqa_k_norm_g[a], one(tc, hd), zero(tc, hd), nb, tc, False, False)

            o_m = _attention(mq, mkc, mk, mvc, mv, MLA_QK_DIM ** -0.5)
            o_g = _attention(gq, gkc, gk, gvc, gv, GQA_HEAD_DIM ** -0.5)
            w_out = attn_w_out[a].astype(BF16)
            n_m = MLA_HEADS * MLA_V_DIM
            x2 = _mm_residual([o_m, o_g], [w_out[:n_m], w_out[n_m:]], x2, mod[2], lat_row)
        else:
            mi = i // 2
            y = _norm_mm(x2, norm1_g[i], mod[0], mod[1], lat_row, conv_w_in[mi].astype(BF16), tn=1280)
            yc = _conv_mixer(y, cfm_dw_w[mi], cfm_dw_b[mi], cfm_ln_g[mi], cfm_ln_b[mi], sc_dw_w[mi], nb, sl)
            x2 = _mm_residual([yc], [conv_w_out[mi].astype(BF16)], x2, mod[2], lat_row)
        x2 = _peer(x2, mod, lat_row, gat_row, norm2_g[i], peer_w_q[i], peer_key1[i], peer_key2[i],
                   _pack_expert_tables(peer_u, peer_v, i), final_norm_g, final_norm=(i == depth - 1))
    return x2.reshape(nb, sl, d)
```
